```python
import math
import jax, jax.numpy as jnp
from jax import lax
import numpy as np

D_MODEL = 2048
BATCH = 1
SEQ = 16384
DEPTH = 1
DEC_BATCH = 8
DEC_SEQ = 64
PAST_LEN = 4096

CHUNK = 64
GMLP_CHUNK = 128
GMLP_WIDTH = D_MODEL
GMLP_GROUPS = 16
GMLP_GDIM = GMLP_WIDTH // GMLP_GROUPS
LRU_WIDTH = D_MODEL
LRU_HEADS = 16
LRU_HDIM = LRU_WIDTH // LRU_HEADS
CONV_W = 4
LRU_C = 8.0
N_KEYS = 128
N_EXPERTS = N_KEYS * N_KEYS
PEER_HEADS = 8
PEER_TOPK = 16
PEER_QDIM = 256
PEER_HALF = PEER_QDIM // 2
PEER_BLOCK = 128
NORM_EPS = 1e-6

OFF_U = 0
OFF_V = OFF_U + GMLP_WIDTH
OFF_X = OFF_V + GMLP_WIDTH
OFF_G = OFF_X + LRU_WIDTH
OFF_GA = OFF_G + LRU_WIDTH
OFF_GB = OFF_GA + D_MODEL
IN_COLS = OFF_GB + D_MODEL

kernel_name = "hybrid_gmlp_rglru_peer_stream_step"


def rms_norm(x, g):
    xf = x.astype(jnp.float32)
    xf = xf * lax.rsqrt(jnp.mean(xf * xf, axis=-1, keepdims=True) + NORM_EPS)
    return (xf * g.astype(jnp.float32)).astype(x.dtype)


def gmlp_spatial(u, v, w_s, b_s):
    B, L, _ = u.shape
    cl = GMLP_CHUNK if L >= GMLP_CHUNK else L
    nc = L // cl
    idx = jnp.arange(GMLP_CHUNK)
    mask = (idx[:, None] // CHUNK) >= (idx[None, :] // CHUNK)
    w = (w_s * mask.astype(w_s.dtype))[:, :cl, :cl]
    vr = v.reshape(B, nc, cl, GMLP_GROUPS, GMLP_GDIM)
    s = jnp.einsum('gij,bcjgd->bcigd', w, vr) + b_s[:, :cl].T[None, None, :, :, None]
    return u * s.reshape(B, L, GMLP_WIDTH)


def causal_conv(xb, prev, w, b):
    L = xb.shape[1]
    xp = jnp.concatenate([prev.astype(xb.dtype), xb], axis=1)
    out = b + sum(xp[:, k:k + L] * w[k] for k in range(CONV_W))
    return out, xp[:, -(CONV_W - 1):]


def _lin_combine(e1, e2):
    a1, b1 = e1
    a2, b2 = e2
    return a1 * a2, a2 * b1 + b2


def rglru(x, h0, lam, w_a, b_a, w_x, b_x):
    B, L, W = x.shape
    xh = x.reshape(B, L, LRU_HEADS, LRU_HDIM)
    r = jax.nn.sigmoid((jnp.einsum('blhi,hij->blhj', xh, w_a).reshape(B, L, W) + b_a).astype(jnp.float32))
    i = jax.nn.sigmoid((jnp.einsum('blhi,hij->blhj', xh, w_x).reshape(B, L, W) + b_x).astype(jnp.float32))
    log_a = -LRU_C * r * jax.nn.softplus(-lam.astype(jnp.float32))
    a = jnp.exp(log_a)
    mult = jnp.sqrt(-jnp.expm1(2.0 * log_a))
    if h0 is None:
        mult = mult.at[:, 0].set(1.0)
    bx = mult * i * x.astype(jnp.float32)
    if h0 is not None:
        bx = bx.at[:, 0].add(a[:, 0] * h0.astype(jnp.float32))
    _, h = lax.associative_scan(_lin_combine, (a, bx), axis=1)
    return h.astype(x.dtype), h[:, -1].astype(x.dtype)


def peer(x, w_q, sub_keys, u_tab, v_tab):
    T = x.shape[0]
    nb = -(-T // PEER_BLOCK)
    xp = jnp.pad(x, ((0, nb * PEER_BLOCK - T), (0, 0))).reshape(nb, PEER_BLOCK, D_MODEL)

    def block(xb):
        q = (xb @ w_q).reshape(PEER_BLOCK, PEER_HEADS, 2, PEER_HALF)
        s = jnp.einsum('thpk,pnk->thpn', q, sub_keys).astype(jnp.float32)
        s1, i1 = lax.top_k(s[:, :, 0], PEER_TOPK)
        s2, i2 = lax.top_k(s[:, :, 1], PEER_TOPK)
        cand = (s1[..., :, None] + s2[..., None, :]).reshape(PEER_BLOCK, PEER_HEADS, PEER_TOPK * PEER_TOPK)
        cidx = (i1[..., :, None] * N_KEYS + i2[..., None, :]).reshape(PEER_BLOCK, PEER_HEADS, PEER_TOPK * PEER_TOPK)
        top, pos = lax.top_k(cand, PEER_TOPK)
        eidx = jnp.take_along_axis(cidx, pos, axis=-1)
        g = jax.nn.softmax(top, axis=-1)
        act = jax.nn.gelu(jnp.einsum('thkd,td->thk', u_tab[eidx], xb).astype(jnp.float32))
        coef = (g * act).astype(xb.dtype)
        return jnp.einsum('thk,thkd->td', coef, v_tab[eidx])

    out = lax.map(block, xp)
    return out.reshape(nb * PEER_BLOCK, D_MODEL)[:T]


def trunk_layer(x, conv_prev, h0, g_mix, w_in, gmlp_ws, gmlp_bs, conv_w, conv_b,
                lru_wa, lru_ba, lru_wx, lru_bx, lru_lambda, w_out, g_ffn,
                peer_wq, peer_keys, peer_u, peer_v):
    B, L, _ = x.shape
    xn = rms_norm(x, g_mix)
    z = xn @ w_in
    u = jax.nn.gelu(z[..., OFF_U:OFF_V])
    v = jax.nn.gelu(z[..., OFF_V:OFF_X])
    zx = z[..., OFF_X:OFF_G]
    zg = z[..., OFF_G:OFF_GA]
    ga = jax.nn.sigmoid(z[..., OFF_GA:OFF_GB].astype(jnp.float32)).astype(x.dtype)
    gb = jax.nn.sigmoid(z[..., OFF_GB:IN_COLS].astype(jnp.float32)).astype(x.dtype)
    y_a = gmlp_spatial(u, v, gmlp_ws, gmlp_bs)
    xc, conv_state = causal_conv(zx, conv_prev, conv_w, conv_b)
    h, h_last = rglru(xc, h0, lru_lambda, lru_wa, lru_ba, lru_wx, lru_bx)
    y_b = h * jax.nn.gelu(zg)
    x = x + (ga * y_a + gb * y_b) @ w_out
    xf = rms_norm(x, g_ffn).reshape(B * L, D_MODEL)
    x = x + peer(xf, peer_wq, peer_keys, peer_u, peer_v).reshape(B, L, D_MODEL)
    return x, conv_state, h_last, v


def setup_inputs(seed: int = 0) -> dict:
    key = jax.random.key(seed)
    ks = jax.random.split(key, 24)
    f32 = jnp.float32
    nrm = lambda k, shape, scale: jax.random.normal(k, shape, f32) * scale
    a_target = jax.random.uniform(ks[13], (DEPTH, LRU_WIDTH), f32, 0.9, 0.999)
    p = a_target ** (1.0 / LRU_C)
    lam = jnp.log(p) - jnp.log1p(-p)
    return {
        "x_prompt": nrm(ks[0], (BATCH, SEQ, D_MODEL), 1.0),
        "x_sample": nrm(ks[1], (DEC_BATCH, DEC_SEQ, D_MODEL), 1.0),
        "state_conv": nrm(ks[2], (DEPTH, DEC_BATCH, CONV_W - 1, LRU_WIDTH), 0.5),
        "state_lru": nrm(ks[3], (DEPTH, DEC_BATCH, LRU_WIDTH), 0.5),
        "g_mix": 1.0 + nrm(ks[4], (DEPTH, D_MODEL), 0.02),
        "w_in": nrm(ks[5], (DEPTH, D_MODEL, IN_COLS), D_MODEL ** -0.5),
        "gmlp_ws": nrm(ks[6], (DEPTH, GMLP_GROUPS, GMLP_CHUNK, GMLP_CHUNK), GMLP_CHUNK ** -0.5),
        "gmlp_bs": 1.0 + nrm(ks[7], (DEPTH, GMLP_GROUPS, GMLP_CHUNK), 0.1),
        "conv_w": nrm(ks[8], (DEPTH, CONV_W, LRU_WIDTH), 0.5),
        "conv_b": nrm(ks[9], (DEPTH, LRU_WIDTH), 0.02),
        "lru_wa": nrm(ks[10], (DEPTH, LRU_HEADS, LRU_HDIM, LRU_HDIM), LRU_HDIM ** -0.5),
        "lru_ba": nrm(ks[11], (DEPTH, LRU_WIDTH), 0.02),
        "lru_wx": nrm(ks[12], (DEPTH, LRU_HEADS, LRU_HDIM, LRU_HDIM), LRU_HDIM ** -0.5),
        "lru_bx": nrm(ks[14], (DEPTH, LRU_WIDTH), 0.02),
        "lru_lambda": lam,
        "w_out": nrm(ks[15], (DEPTH, D_MODEL, D_MODEL), D_MODEL ** -0.5),
        "g_ffn": 1.0 + nrm(ks[16], (DEPTH, D_MODEL), 0.02),
        "peer_wq": nrm(ks[17], (DEPTH, D_MODEL, PEER_HEADS * PEER_QDIM), D_MODEL ** -0.5),
        "peer_keys": nrm(ks[18], (DEPTH, 2, N_KEYS, PEER_HALF), PEER_HALF ** -0.5),
        "peer_u": nrm(ks[19], (DEPTH, N_EXPERTS, D_MODEL), D_MODEL ** -0.5),
        "peer_v": nrm(ks[20], (DEPTH, N_EXPERTS, D_MODEL), PEER_HEADS ** -0.5),
        "g_final": 1.0 + nrm(ks[21], (D_MODEL,), 0.02),
    }


def reference(x_prompt, x_sample, state_conv, state_lru, g_mix, w_in, gmlp_ws, gmlp_bs,
              conv_w, conv_b, lru_wa, lru_ba, lru_wx, lru_bx, lru_lambda, w_out, g_ffn,
              peer_wq, peer_keys, peer_u, peer_v, g_final):
    hp = x_prompt
    hs = x_sample
    conv_p, lru_p, conv_s, lru_s, vrows_s = [], [], [], [], []
    for l in range(DEPTH):
        w = (g_mix[l], w_in[l], gmlp_ws[l], gmlp_bs[l], conv_w[l], conv_b[l],
             lru_wa[l], lru_ba[l], lru_wx[l], lru_bx[l], lru_lambda[l], w_out[l], g_ffn[l],
             peer_wq[l], peer_keys[l], peer_u[l], peer_v[l])
        zero_prev = jnp.zeros((hp.shape[0], CONV_W - 1, LRU_WIDTH), hp.dtype)
        hp, cp, lp, _ = trunk_layer(hp, zero_prev, None, *w)
        hs, cs, ls, vs = trunk_layer(hs, state_conv[l], state_lru[l], *w)
        conv_p.append(cp)
        lru_p.append(lp)
        conv_s.append(cs)
        lru_s.append(ls)
        vrows_s.append(vs)
    y_prompt = rms_norm(hp, g_final)
    y_sample = rms_norm(hs, g_final)
    conv_prompt = jnp.stack(conv_p, axis=0)
    lru_prompt = jnp.stack(lru_p, axis=0)
    conv_sample = jnp.stack(conv_s, axis=0)
    lru_sample = jnp.stack(lru_s, axis=0)
    gmlp_v_sample = jnp.stack(vrows_s, axis=0)
    return (y_prompt, y_sample, conv_prompt, lru_prompt, conv_sample, lru_sample, gmlp_v_sample)
```

```python
import functools
import math

import jax
import jax.numpy as jnp
from jax import lax
from jax.experimental import pallas as pl
from jax.experimental.pallas import tpu as pltpu

F32 = jnp.float32
BF16 = jnp.bfloat16

NORM_EPS = 1e-6
N_PARTS = 6
LANES = 128
SUBLANES = 8
CONV_W = 4
LRU_C = 8.0
GMLP_CHUNK = 128
CAUSAL_CHUNK = 64
N_KEYS = 128
PEER_HEADS = 8
PEER_TOPK = 16
VMEM_LIMIT = 56 * 1024 * 1024


def _gelu(x):
    c = math.sqrt(2.0 / math.pi)
    return x * (0.5 * (1.0 + jnp.tanh(c * (x + 0.044715 * (x * x * x)))))


def _rms_norm(x, g):
    ms = jnp.mean(x * x, axis=-1, keepdims=True)
    return (x * lax.rsqrt(ms + NORM_EPS)) * g


def _inproj_body(x_ref, g_ref, w_ref, z_ref, xn_ref, *, blocks_per_part):
    j = pl.program_id(1)

    @pl.when(j == 0)
    def _():
        xn_ref[...] = _rms_norm(x_ref[...], g_ref[...]).astype(BF16)

    z = jnp.dot(xn_ref[...], w_ref[...], preferred_element_type=F32)
    part = j // blocks_per_part
    is_gelu = jnp.logical_or(part <= 1, part == 3)

    @pl.when(is_gelu)
    def _():
        z_ref[...] = _gelu(z)

    @pl.when(part == 2)
    def _():
        z_ref[...] = z

    @pl.when(part >= 4)
    def _():
        z_ref[...] = jax.nn.sigmoid(z)


def _inproj(x, g, w_bf, *, tm, tn):
    t, d = x.shape
    n = w_bf.shape[1]
    return pl.pallas_call(
        functools.partial(_inproj_body, blocks_per_part=(n // N_PARTS) // tn),
        grid=(t // tm, n // tn),
        in_specs=[
            pl.BlockSpec((tm, d), lambda i, j: (i, 0)),
            pl.BlockSpec((1, d), lambda i, j: (0, 0)),
            pl.BlockSpec((d, tn), lambda i, j: (0, j)),
        ],
        out_specs=pl.BlockSpec((tm, tn), lambda i, j: (i, j)),
        out_shape=jax.ShapeDtypeStruct((t, n), F32),
        scratch_shapes=[pltpu.VMEM((tm, d), BF16)],
        compiler_params=pltpu.CompilerParams(
            dimension_semantics=("arbitrary", "arbitrary"), vmem_limit_bytes=VMEM_LIMIT),
        name="inproj",
    )(x, g.reshape(1, d), w_bf)


def _shift_rows(x, s, fill):
    rolled = pltpu.roll(x, s, axis=0)
    rows = lax.broadcasted_iota(jnp.int32, x.shape, 0)
    return jnp.where(rows >= s, rolled, fill)


def _mixer_body(zu_ref, zv_ref, zx_ref, zg_ref, zga_ref, zgb_ref, x_ref,
                wg_ref, bg_ref, cw_ref, cb_ref, wa_ref, ba_ref, wx_ref, bx_ref, lam_ref,
                wout_ref, prev_ref, h0_ref,
                x2_ref, conv_ref, hlast_ref,
                xp_ref, merged_ref, hc_ref, *, carried):
    c = pl.program_id(0)
    rows_c = zx_ref.shape[0]
    pad = SUBLANES

    if carried:
        @pl.when(c == 0)
        def _():
            xp_ref[0:pad, :] = jnp.zeros((pad, xp_ref.shape[1]), F32)
            hc_ref[...] = jnp.zeros(hc_ref.shape, F32)
    else:
        xp_ref[pad - (CONV_W - 1):pad, :] = prev_ref[0]
        hc_ref[0:1, :] = h0_ref[0]
    xp_ref[pad:pad + rows_c, :] = zx_ref[...]

    def group(j, carry):
        col = pl.ds(pl.multiple_of(j * LANES, LANES), LANES)
        s = jnp.dot(wg_ref[j], zv_ref[:, col].astype(BF16), preferred_element_type=F32) + bg_ref[:, col]
        y_a = zu_ref[:, col] * s
        xc = cb_ref[:, col]
        for k in range(CONV_W):
            lo = pad - (CONV_W - 1) + k
            xc = xc + xp_ref[lo:lo + rows_c, col] * cw_ref[k:k + 1, col]
        xcb = xc.astype(BF16)
        r = jax.nn.sigmoid(jnp.dot(xcb, wa_ref[j], preferred_element_type=F32) + ba_ref[:, col])
        i = jax.nn.sigmoid(jnp.dot(xcb, wx_ref[j], preferred_element_type=F32) + bx_ref[:, col])
        nlam = -lam_ref[:, col]
        softplus = jnp.maximum(nlam, 0.0) + jnp.log1p(jnp.exp(-jnp.abs(nlam)))
        log_a = (-LRU_C) * r * softplus
        a = jnp.exp(log_a)
        mult = jnp.sqrt(jnp.tanh(-log_a) * (a * a + 1.0))
        if carried:
            rows = lax.broadcasted_iota(jnp.int32, mult.shape, 0)
            mult = jnp.where(jnp.logical_and(rows == 0, c == 0), 1.0, mult)
        b = mult * i * xc
        s_len = 1
        while s_len < rows_c:
            a_sh = _shift_rows(a, s_len, 1.0)
            b_sh = _shift_rows(b, s_len, 0.0)
            b = a * b_sh + b
            a = a * a_sh
            s_len *= 2
        h = a * hc_ref[0:1, col] + b
        hc_ref[0:1, col] = h[rows_c - 1:rows_c, :]
        y_b = h * zg_ref[:, col]
        merged_ref[:, col] = (zga_ref[:, col] * y_a + zgb_ref[:, col] * y_b).astype(BF16)
        return carry

    lax.fori_loop(0, zx_ref.shape[1] // LANES, group, 0)

    x2_ref[...] = x_ref[...] + jnp.dot(merged_ref[...], wout_ref[...], preferred_element_type=F32)
    conv_ref[0] = xp_ref[pad + rows_c - (CONV_W - 1):pad + rows_c, :]
    hlast_ref[0] = hc_ref[0:1, :]
    if carried:
        xp_ref[0:pad, :] = xp_ref[rows_c:rows_c + pad, :]


def _mixer(z, x, row0, n_seq, rows_c, wg, bg, cw, cb, wa, ba, wx, bx, lam, wout, prev, h0, *, carried):
    d = x.shape[1]
    blk0 = row0 // rows_c
    n_state = prev.shape[0]

    def zspec(part):
        return pl.BlockSpec((rows_c, d), lambda c, part=part: (blk0 + c, part))

    def const(shape):
        return pl.BlockSpec(shape, lambda c: (0,) * len(shape))

    def state_idx(c):
        return (c, 0, 0) if not carried else (0, 0, 0)

    return pl.pallas_call(
        functools.partial(_mixer_body, carried=carried),
        grid=(n_seq,),
        in_specs=[zspec(p) for p in range(N_PARTS)] + [
            pl.BlockSpec((rows_c, d), lambda c: (blk0 + c, 0)),
            const(wg.shape), const(bg.shape), const(cw.shape), const(cb.shape),
            const(wa.shape), const(ba.shape), const(wx.shape), const(bx.shape), const(lam.shape),
            const(wout.shape),
            pl.BlockSpec((1, CONV_W - 1, d), state_idx),
            pl.BlockSpec((1, 1, d), state_idx),
        ],
        out_specs=[
            pl.BlockSpec((rows_c, d), lambda c: (c, 0)),
            pl.BlockSpec((1, CONV_W - 1, d), state_idx),
            pl.BlockSpec((1, 1, d), state_idx),
        ],
        out_shape=[
            jax.ShapeDtypeStruct((n_seq * rows_c, d), F32),
            jax.ShapeDtypeStruct((n_state, CONV_W - 1, d), F32),
            jax.ShapeDtypeStruct((n_state, 1, d), F32),
        ],
        scratch_shapes=[
            pltpu.VMEM((rows_c + 2 * SUBLANES, d), F32),
            pltpu.VMEM((rows_c, d), BF16),
            pltpu.VMEM((SUBLANES, d), F32),
        ],
        compiler_params=pltpu.CompilerParams(
            dimension_semantics=("arbitrary",), vmem_limit_bytes=VMEM_LIMIT),
        name="mixer_stream" if carried else "mixer_step",
    )(z, z, z, z, z, z, x, wg, bg, cw, cb, wa, ba, wx, bx, lam, wout, prev, h0)


def _sort_network(n):
    pairs = []
    p = 1
    while p < n:
        k = p
        while k >= 1:
            for j in range(k % p, n - k, 2 * k):
                for i in range(min(k, n - j - k)):
                    if (i + j) // (2 * p) == (i + j + k) // (2 * p):
                        pairs.append((i + j, i + j + k))
            k //= 2
        p *= 2
    return pairs


_SORT16 = _sort_network(PEER_TOPK)


def _sort_desc(v):
    v = list(v)
    for i, j in _SORT16:
        v[i], v[j] = jnp.maximum(v[i], v[j]), jnp.minimum(v[i], v[j])
    return v


def _merge_top(xs, ys):
    z = [jnp.maximum(xs[i], ys[PEER_TOPK - 1 - i]) for i in range(PEER_TOPK)]
    d = PEER_TOPK // 2
    while d >= 1:
        for i in range(PEER_TOPK):
            if (i & d) == 0:
                z[i], z[i + d] = jnp.maximum(z[i], z[i + d]), jnp.minimum(z[i], z[i + d])
        d //= 2
    return z


def _top_of_lists(lists):
    while len(lists) > 1:
        lists = [_merge_top(lists[i], lists[i + 1]) for i in range(0, len(lists), 2)]
    return lists[0]


def _route_body(x2_ref, g_ref, wqt_ref, keys_ref, xf_ref, e1_ref, e2_ref, th_ref, s_ref):
    tm = x2_ref.shape[0]
    hk = PEER_HEADS * N_KEYS
    xf = _rms_norm(x2_ref[...], g_ref[...]).astype(BF16)
    xf_ref[...] = xf
    qt = lax.dot_general(wqt_ref[...], xf, (((1,), (1,)), ((), ())), preferred_element_type=F32)
    for p in range(2):
        for h in range(PEER_HEADS):
            lo = p * hk + h * N_KEYS
            q_hp = qt[lo:lo + N_KEYS, :].astype(BF16)
            s_hp = jnp.dot(keys_ref[p], q_hp, preferred_element_type=F32)
            for g in range(tm // LANES):
                s_ref[p, g, h * N_KEYS:(h + 1) * N_KEYS, :] = s_hp[:, g * LANES:(g + 1) * LANES]

    def lane_group(lg, carry):
        lanes = pl.ds(pl.multiple_of(lg * LANES, LANES), LANES)
        tops = []
        for p in range(2):
            vals = [s_ref[p, lg, pl.ds(n, PEER_HEADS, stride=N_KEYS), :] for n in range(N_KEYS)]
            groups = [_sort_desc(vals[i:i + PEER_TOPK]) for i in range(0, N_KEYS, PEER_TOPK)]
            tops.append(_top_of_lists(groups))
        m1, m2 = tops[0][0], tops[1][0]
        e1 = [jnp.exp(v - m1) for v in tops[0]]
        e2 = [jnp.exp(v - m2) for v in tops[1]]
        best = _top_of_lists([[e1[i] * e2[j] for j in range(PEER_TOPK)] for i in range(PEER_TOPK)])
        z = best[0]
        for v in best[1:]:
            z = z + v
        rz = 1.0 / z
        e1n = [v * rz for v in e1]
        bestn = _top_of_lists([[e1n[i] * e2[j] for j in range(PEER_TOPK)] for i in range(PEER_TOPK)])
        th_ref[:, lanes] = bestn[PEER_TOPK - 1]
        for h in range(PEER_HEADS):
            rows = slice(h * N_KEYS, (h + 1) * N_KEYS)
            e1_ref[h, :, lanes] = jnp.exp(s_ref[0, lg, rows, :] - m1[h:h + 1, :]) * rz[h:h + 1, :]
            e2_ref[h, :, lanes] = jnp.exp(s_ref[1, lg, rows, :] - m2[h:h + 1, :])
        return carry

    lax.fori_loop(0, tm // LANES, lane_group, 0)


def _route(x2, g, wqt, keys_bf, *, tm):
    t, d = x2.shape
    hk = PEER_HEADS * N_KEYS
    return pl.pallas_call(
        _route_body,
        grid=(t // tm,),
        in_specs=[
            pl.BlockSpec((tm, d), lambda i: (i, 0)),
            pl.BlockSpec((1, d), lambda i: (0, 0)),
            pl.BlockSpec(wqt.shape, lambda i: (0, 0)),
            pl.BlockSpec(keys_bf.shape, lambda i: (0, 0, 0)),
        ],
        out_specs=[
            pl.BlockSpec((tm, d), lambda i: (i, 0)),
            pl.BlockSpec((PEER_HEADS, N_KEYS, tm), lambda i: (0, 0, i)),
            pl.BlockSpec((PEER_HEADS, N_KEYS, tm), lambda i: (0, 0, i)),
            pl.BlockSpec((PEER_HEADS, tm), lambda i: (0, i)),
        ],
        out_shape=[
            jax.ShapeDtypeStruct((t, d), BF16),
            jax.ShapeDtypeStruct((PEER_HEADS, N_KEYS, t), F32),
            jax.ShapeDtypeStruct((PEER_HEADS, N_KEYS, t), F32),
            jax.ShapeDtypeStruct((PEER_HEADS, t), F32),
        ],
        scratch_shapes=[pltpu.VMEM((2, tm // LANES, hk, LANES), F32)],
        compiler_params=pltpu.CompilerParams(
            dimension_semantics=("arbitrary",), vmem_limit_bytes=VMEM_LIMIT),
        name="peer_route",
    )(x2, g.reshape(1, d), wqt, keys_bf)


def _experts_body(xf_ref, u_ref, vt_ref, e1_ref, e2_ref, th_ref, x2_ref, gfin_ref,
                  y_ref, acc_ref, *, final_norm):
    j = pl.program_id(1)
    te = u_ref.shape[0]
    keys_per_step = te // N_KEYS

    @pl.when(j == 0)
    def _():
        acc_ref[...] = jnp.zeros(acc_ref.shape, F32)

    act = _gelu(lax.dot_general(u_ref[...], xf_ref[...], (((1,), (1,)), ((), ())),
                                preferred_element_type=F32))
    coefs = []
    for k in range(keys_per_step):
        a = j * keys_per_step + k
        gate = None
        for h in range(PEER_HEADS):
            prod = e1_ref[h, pl.ds(a, 1), :] * e2_ref[h]
            sel = jnp.where(prod >= th_ref[h:h + 1, :], prod, 0.0)
            gate = sel if gate is None else gate + sel
        coefs.append((gate * act[k * N_KEYS:(k + 1) * N_KEYS, :]).astype(BF16))
    coef = coefs[0] if keys_per_step == 1 else jnp.concatenate(coefs, axis=0)
    acc_ref[...] += jnp.dot(vt_ref[...], coef, preferred_element_type=F32)

    @pl.when(j == pl.num_programs(1) - 1)
    def _():
        x3 = x2_ref[...] + acc_ref[...].T
        y_ref[...] = _rms_norm(x3, gfin_ref[...]) if final_norm else x3


def _experts(xf, u_bf, vt_bf, e1, e2, th, x2, gfin, *, tm, te, final_norm):
    t, d = x2.shape
    n_exp = u_bf.shape[0]
    return pl.pallas_call(
        functools.partial(_experts_body, final_norm=final_norm),
        grid=(t // tm, n_exp // te),
        in_specs=[
            pl.BlockSpec((tm, d), lambda i, j: (i, 0)),
            pl.BlockSpec((te, d), lambda i, j: (j, 0)),
            pl.BlockSpec((d, te), lambda i, j: (0, j)),
            pl.BlockSpec((PEER_HEADS, N_KEYS, tm), lambda i, j: (0, 0, i)),
            pl.BlockSpec((PEER_HEADS, N_KEYS, tm), lambda i, j: (0, 0, i)),
            pl.BlockSpec((PEER_HEADS, tm), lambda i, j: (0, i)),
            pl.BlockSpec((tm, d), lambda i, j: (i, 0)),
            pl.BlockSpec((1, d), lambda i, j: (0, 0)),
        ],
        out_specs=pl.BlockSpec((tm, d), lambda i, j: (i, 0)),
        out_shape=jax.ShapeDtypeStruct((t, d), F32),
        scratch_shapes=[pltpu.VMEM((d, tm), F32)],
        compiler_params=pltpu.CompilerParams(
            dimension_semantics=("arbitrary", "arbitrary"), vmem_limit_bytes=VMEM_LIMIT),
        name="peer_experts",
    )(xf, u_bf, vt_bf, e1, e2, th, x2, gfin.reshape(1, d))


def _pick_tile(n, pref):
    t = pref
    while n % t:
        t //= 2
    return t


def kernel(x_prompt, x_sample, state_conv, state_lru, g_mix, w_in, gmlp_ws, gmlp_bs, conv_w, conv_b,
           lru_wa, lru_ba, lru_wx, lru_bx, lru_lambda, w_out, g_ffn, peer_wq, peer_keys, peer_u,
           peer_v, g_final):
    batch, seq, d = x_prompt.shape
    dec_batch, dec_seq, _ = x_sample.shape
    depth = w_in.shape[0]
    assert batch == 1 and seq % GMLP_CHUNK == 0 and dec_seq <= CAUSAL_CHUNK
    assert d // LANES == gmlp_ws.shape[1] == lru_wa.shape[1]
    tp, ts = batch * seq, dec_batch * dec_seq
    t = tp + ts
    assert tp % dec_seq == 0

    x = jnp.concatenate([x_prompt.reshape(tp, d), x_sample.reshape(ts, d)], axis=0)
    tm = _pick_tile(t, 512)

    idx = jnp.arange(GMLP_CHUNK)
    mask = (idx[:, None] // CAUSAL_CHUNK) >= (idx[None, :] // CAUSAL_CHUNK)
    hk = PEER_HEADS * N_KEYS

    conv_p, lru_p, conv_s, lru_s, vrows_s = [], [], [], [], []
    for l in range(depth):
        z = _inproj(x, g_mix[l], w_in[l].astype(BF16), tm=tm, tn=1024)

        ws_p = (gmlp_ws[l] * mask.astype(F32)).astype(BF16)
        ws_s = ws_p[:, :dec_seq, :dec_seq]
        bias = jnp.repeat(gmlp_bs[l].T, LANES, axis=1)
        shared = (conv_w[l], conv_b[l].reshape(1, d), lru_wa[l].astype(BF16), lru_ba[l].reshape(1, d),
                  lru_wx[l].astype(BF16), lru_bx[l].reshape(1, d), lru_lambda[l].reshape(1, d),
                  w_out[l].astype(BF16))
        zero_prev = jnp.zeros((1, CONV_W - 1, d), F32)
        zero_h = jnp.zeros((1, 1, d), F32)
        x2p, cp, lp = _mixer(z, x, 0, tp // GMLP_CHUNK, GMLP_CHUNK, ws_p, bias, *shared,
                             zero_prev, zero_h, carried=True)
        x2s, cs, ls = _mixer(z, x, tp, dec_batch, dec_seq, ws_s, bias[:dec_seq], *shared,
                             state_conv[l], state_lru[l].reshape(dec_batch, 1, d), carried=False)
        x2 = jnp.concatenate([x2p, x2s], axis=0)

        wqt = peer_wq[l].reshape(d, PEER_HEADS, 2, N_KEYS).transpose(2, 1, 3, 0).reshape(2 * hk, d)
        xf, e1, e2, th = _route(x2, g_ffn[l], wqt.astype(BF16), peer_keys[l].astype(BF16),
                                tm=_pick_tile(t, 256))
        last = l == depth - 1
        x = _experts(xf, peer_u[l].astype(BF16), peer_v[l].T.astype(BF16), e1, e2, th, x2,
                     g_final if last else jnp.ones((d,), F32), tm=tm, te=512, final_norm=last)

        conv_p.append(cp)
        lru_p.append(lp.reshape(batch, d))
        conv_s.append(cs)
        lru_s.append(ls.reshape(dec_batch, d))
        vrows_s.append(z[tp:, d:2 * d].reshape(dec_batch, dec_seq, d))

    y_prompt = x[:tp].reshape(batch, seq, d)
    y_sample = x[tp:].reshape(dec_batch, dec_seq, d)
    return (y_prompt, y_sample, jnp.stack(conv_p), jnp.stack(lru_p), jnp.stack(conv_s),
            jnp.stack(lru_s), jnp.stack(vrows_s))
```

```python
import functools
import math

import jax
import jax.numpy as jnp
from jax import lax
from jax.experimental import pallas as pl
from jax.experimental.pallas import tpu as pltpu

F32 = jnp.float32
BF16 = jnp.bfloat16

NORM_EPS = 1e-6
N_PARTS = 6
LANES = 128
SUBLANES = 8
CONV_W = 4
LRU_C = 8.0
GMLP_CHUNK = 128
CAUSAL_CHUNK = 64
N_KEYS = 128
PEER_HEADS = 8
PEER_TOPK = 16
VMEM_LIMIT = 56 * 1024 * 1024
ACC_SLAB = 512


def _gelu(x):
    c = math.sqrt(2.0 / math.pi)
    return x * (0.5 * (1.0 + jnp.tanh(c * (x + 0.044715 * (x * x * x)))))


def _rms_norm(x, g):
    ms = jnp.mean(x * x, axis=-1, keepdims=True)
    return (x * lax.rsqrt(ms + NORM_EPS)) * g


def _inproj_body(x_ref, g_ref, w_ref, z_ref, xn_ref, *, blocks_per_part):
    j = pl.program_id(1)

    @pl.when(j == 0)
    def _():
        xn_ref[...] = _rms_norm(x_ref[...], g_ref[...]).astype(BF16)

    z = jnp.dot(xn_ref[...], w_ref[...], preferred_element_type=F32)
    part = j // blocks_per_part
    is_gelu = jnp.logical_or(part <= 1, part == 3)

    @pl.when(is_gelu)
    def _():
        z_ref[...] = _gelu(z)

    @pl.when(part == 2)
    def _():
        z_ref[...] = z

    @pl.when(part >= 4)
    def _():
        z_ref[...] = jax.nn.sigmoid(z)


def _inproj(x, g, w_bf, *, tm, tn):
    t, d = x.shape
    n = w_bf.shape[1]
    return pl.pallas_call(
        functools.partial(_inproj_body, blocks_per_part=(n // N_PARTS) // tn),
        grid=(t // tm, n // tn),
        in_specs=[
            pl.BlockSpec((tm, d), lambda i, j: (i, 0)),
            pl.BlockSpec((1, d), lambda i, j: (0, 0)),
            pl.BlockSpec((d, tn), lambda i, j: (0, j)),
        ],
        out_specs=pl.BlockSpec((tm, tn), lambda i, j: (i, j)),
        out_shape=jax.ShapeDtypeStruct((t, n), F32),
        scratch_shapes=[pltpu.VMEM((tm, d), BF16)],
        compiler_params=pltpu.CompilerParams(
            dimension_semantics=("arbitrary", "arbitrary"), vmem_limit_bytes=VMEM_LIMIT),
        name="inproj",
    )(x, g.reshape(1, d), w_bf)


def _shift_rows(x, s, fill):
    rolled = pltpu.roll(x, s, axis=0)
    rows = lax.broadcasted_iota(jnp.int32, x.shape, 0)
    return jnp.where(rows >= s, rolled, fill)


def _mixer_body(zu_ref, zv_ref, zx_ref, zg_ref, zga_ref, zgb_ref, x_ref,
                wg_ref, bg_ref, cw_ref, cb_ref, wa_ref, ba_ref, wx_ref, bx_ref, lam_ref,
                wout_ref, prev_ref, h0_ref,
                x2_ref, conv_ref, hlast_ref,
                xp_ref, merged_ref, hc_ref, *, carried):
    c = pl.program_id(0)
    rows_c = zx_ref.shape[0]
    pad = SUBLANES

    if carried:
        @pl.when(c == 0)
        def _():
            xp_ref[0:pad, :] = jnp.zeros((pad, xp_ref.shape[1]), F32)
            hc_ref[...] = jnp.zeros(hc_ref.shape, F32)
    else:
        xp_ref[pad - (CONV_W - 1):pad, :] = prev_ref[0]
        hc_ref[0:1, :] = h0_ref[0]
    xp_ref[pad:pad + rows_c, :] = zx_ref[...]

    def group(j, carry):
        col = pl.ds(pl.multiple_of(j * LANES, LANES), LANES)
        s = jnp.dot(wg_ref[j], zv_ref[:, col].astype(BF16), preferred_element_type=F32) + bg_ref[:, col]
        y_a = zu_ref[:, col] * s
        xc = cb_ref[:, col]
        for k in range(CONV_W):
            lo = pad - (CONV_W - 1) + k
            xc = xc + xp_ref[lo:lo + rows_c, col] * cw_ref[k:k + 1, col]
        xcb = xc.astype(BF16)
        r = jax.nn.sigmoid(jnp.dot(xcb, wa_ref[j], preferred_element_type=F32) + ba_ref[:, col])
        i = jax.nn.sigmoid(jnp.dot(xcb, wx_ref[j], preferred_element_type=F32) + bx_ref[:, col])
        nlam = -lam_ref[:, col]
        softplus = jnp.maximum(nlam, 0.0) + jnp.log1p(jnp.exp(-jnp.abs(nlam)))
        log_a = (-LRU_C) * r * softplus
        a = jnp.exp(log_a)
        mult = jnp.sqrt(jnp.tanh(-log_a) * (a * a + 1.0))
        if carried:
            rows = lax.broadcasted_iota(jnp.int32, mult.shape, 0)
            mult = jnp.where(jnp.logical_and(rows == 0, c == 0), 1.0, mult)
        b = mult * i * xc
        s_len = 1
        while s_len < rows_c:
            a_sh = _shift_rows(a, s_len, 1.0)
            b_sh = _shift_rows(b, s_len, 0.0)
            b = a * b_sh + b
            a = a * a_sh
            s_len *= 2
        h = a * hc_ref[0:1, col] + b
        hc_ref[0:1, col] = h[rows_c - 1:rows_c, :]
        y_b = h * zg_ref[:, col]
        merged_ref[:, col] = (zga_ref[:, col] * y_a + zgb_ref[:, col] * y_b).astype(BF16)
        return carry

    lax.fori_loop(0, zx_ref.shape[1] // LANES, group, 0)

    x2_ref[...] = x_ref[...] + jnp.dot(merged_ref[...], wout_ref[...], preferred_element_type=F32)
    conv_ref[0] = xp_ref[pad + rows_c - (CONV_W - 1):pad + rows_c, :]
    hlast_ref[0] = hc_ref[0:1, :]
    if carried:
        xp_ref[0:pad, :] = xp_ref[rows_c:rows_c + pad, :]


def _mixer(z, x, row0, n_seq, rows_c, wg, bg, cw, cb, wa, ba, wx, bx, lam, wout, prev, h0, *, carried):
    d = x.shape[1]
    blk0 = row0 // rows_c
    n_state = prev.shape[0]

    def zspec(part):
        return pl.BlockSpec((rows_c, d), lambda c, part=part: (blk0 + c, part))

    def const(shape):
        return pl.BlockSpec(shape, lambda c: (0,) * len(shape))

    def state_idx(c):
        return (c, 0, 0) if not carried else (0, 0, 0)

    return pl.pallas_call(
        functools.partial(_mixer_body, carried=carried),
        grid=(n_seq,),
        in_specs=[zspec(p) for p in range(N_PARTS)] + [
            pl.BlockSpec((rows_c, d), lambda c: (blk0 + c, 0)),
            const(wg.shape), const(bg.shape), const(cw.shape), const(cb.shape),
            const(wa.shape), const(ba.shape), const(wx.shape), const(bx.shape), const(lam.shape),
            const(wout.shape),
            pl.BlockSpec((1, CONV_W - 1, d), state_idx),
            pl.BlockSpec((1, 1, d), state_idx),
        ],
        out_specs=[
            pl.BlockSpec((rows_c, d), lambda c: (c, 0)),
            pl.BlockSpec((1, CONV_W - 1, d), state_idx),
            pl.BlockSpec((1, 1, d), state_idx),
        ],
        out_shape=[
            jax.ShapeDtypeStruct((n_seq * rows_c, d), F32),
            jax.ShapeDtypeStruct((n_state, CONV_W - 1, d), F32),
            jax.ShapeDtypeStruct((n_state, 1, d), F32),
        ],
        scratch_shapes=[
            pltpu.VMEM((rows_c + 2 * SUBLANES, d), F32),
            pltpu.VMEM((rows_c, d), BF16),
            pltpu.VMEM((SUBLANES, d), F32),
        ],
        compiler_params=pltpu.CompilerParams(
            dimension_semantics=("arbitrary",), vmem_limit_bytes=VMEM_LIMIT),
        name="mixer_stream" if carried else "mixer_step",
    )(z, z, z, z, z, z, x, wg, bg, cw, cb, wa, ba, wx, bx, lam, wout, prev, h0)


def _sort_network(n):
    pairs = []
    p = 1
    while p < n:
        k = p
        while k >= 1:
            for j in range(k % p, n - k, 2 * k):
                for i in range(min(k, n - j - k)):
                    if (i + j) // (2 * p) == (i + j + k) // (2 * p):
                        pairs.append((i + j, i + j + k))
            k //= 2
        p *= 2
    return pairs


_SORT16 = _sort_network(PEER_TOPK)


def _sort_desc(v):
    v = list(v)
    for i, j in _SORT16:
        v[i], v[j] = jnp.maximum(v[i], v[j]), jnp.minimum(v[i], v[j])
    return v


def _merge_top(xs, ys):
    z = [jnp.maximum(xs[i], ys[PEER_TOPK - 1 - i]) for i in range(PEER_TOPK)]
    d = PEER_TOPK // 2
    while d >= 1:
        for i in range(PEER_TOPK):
            if (i & d) == 0:
                z[i], z[i + d] = jnp.maximum(z[i], z[i + d]), jnp.minimum(z[i], z[i + d])
        d //= 2
    return z


def _top_of_lists(lists):
    while len(lists) > 1:
        lists = [_merge_top(lists[i], lists[i + 1]) for i in range(0, len(lists), 2)]
    return lists[0]


def _route_body(x2_ref, g_ref, wqt_ref, keys_ref, xft_ref, e1_ref, e2_ref, th_ref, s_ref):
    tm = x2_ref.shape[0]
    hk = PEER_HEADS * N_KEYS
    xft = _rms_norm(x2_ref[...], g_ref[...]).T.astype(BF16)
    xft_ref[0] = xft
    qt = jnp.dot(wqt_ref[...], xft, preferred_element_type=F32)
    for p in range(2):
        for h in range(PEER_HEADS):
            lo = p * hk + h * N_KEYS
            q_hp = qt[lo:lo + N_KEYS, :].astype(BF16)
            s_hp = jnp.dot(keys_ref[p], q_hp, preferred_element_type=F32)
            for g in range(tm // LANES):
                s_ref[p, g, h * N_KEYS:(h + 1) * N_KEYS, :] = s_hp[:, g * LANES:(g + 1) * LANES]

    def lane_group(lg, carry):
        lanes = pl.ds(pl.multiple_of(lg * LANES, LANES), LANES)
        tops = []
        for p in range(2):
            vals = [s_ref[p, lg, pl.ds(n, PEER_HEADS, stride=N_KEYS), :] for n in range(N_KEYS)]
            groups = [_sort_desc(vals[i:i + PEER_TOPK]) for i in range(0, N_KEYS, PEER_TOPK)]
            tops.append(_top_of_lists(groups))
        m1, m2 = tops[0][0], tops[1][0]
        e1 = [jnp.exp(v - m1) for v in tops[0]]
        e2 = [jnp.exp(v - m2) for v in tops[1]]
        best = _top_of_lists([[e1[i] * e2[j] for j in range(PEER_TOPK)] for i in range(PEER_TOPK)])
        z = best[0]
        for v in best[1:]:
            z = z + v
        rz = 1.0 / z
        e1n = [v * rz for v in e1]
        bestn = _top_of_lists([[e1n[i] * e2[j] for j in range(PEER_TOPK)] for i in range(PEER_TOPK)])
        th_ref[0, :, lanes] = bestn[PEER_TOPK - 1]
        for h in range(PEER_HEADS):
            rows = slice(h * N_KEYS, (h + 1) * N_KEYS)
            e1_ref[0, h, :, lanes] = jnp.exp(s_ref[0, lg, rows, :] - m1[h:h + 1, :]) * rz[h:h + 1, :]
            e2_ref[0, h, :, lanes] = jnp.exp(s_ref[1, lg, rows, :] - m2[h:h + 1, :])
        return carry

    lax.fori_loop(0, tm // LANES, lane_group, 0)


def _route(x2, g, wqt, keys_bf, *, tm):
    t, d = x2.shape
    hk = PEER_HEADS * N_KEYS
    n_groups = t // tm
    return pl.pallas_call(
        _route_body,
        grid=(t // tm,),
        in_specs=[
            pl.BlockSpec((tm, d), lambda i: (i, 0)),
            pl.BlockSpec((1, d), lambda i: (0, 0)),
            pl.BlockSpec(wqt.shape, lambda i: (0, 0)),
            pl.BlockSpec(keys_bf.shape, lambda i: (0, 0, 0)),
        ],
        out_specs=[
            pl.BlockSpec((1, d, tm), lambda i: (i, 0, 0)),
            pl.BlockSpec((1, PEER_HEADS, N_KEYS, tm), lambda i: (i, 0, 0, 0)),
            pl.BlockSpec((1, PEER_HEADS, N_KEYS, tm), lambda i: (i, 0, 0, 0)),
            pl.BlockSpec((1, PEER_HEADS, tm), lambda i: (i, 0, 0)),
        ],
        out_shape=[
            jax.ShapeDtypeStruct((n_groups, d, tm), BF16),
            jax.ShapeDtypeStruct((n_groups, PEER_HEADS, N_KEYS, tm), F32),
            jax.ShapeDtypeStruct((n_groups, PEER_HEADS, N_KEYS, tm), F32),
            jax.ShapeDtypeStruct((n_groups, PEER_HEADS, tm), F32),
        ],
        scratch_shapes=[pltpu.VMEM((2, tm // LANES, hk, LANES), F32)],
        compiler_params=pltpu.CompilerParams(
            dimension_semantics=("arbitrary",), vmem_limit_bytes=VMEM_LIMIT),
        name="peer_route",
    )(x2, g.reshape(1, d), wqt, keys_bf)


def _fold_rows_min(x, rows):
    while x.shape[0] > rows:
        half = x.shape[0] // 2
        x = jnp.minimum(x[:half], x[half:])
    return x


def _experts_body(xft_ref, u_ref, vt_ref, e1_ref, e2_ref, th_ref, x2_ref, gfin_ref,
                  y_ref, acc_ref, gate_ref, coef_even_ref, coef_odd_ref, *, final_norm):
    j = pl.program_id(1)
    n_tiles = pl.num_programs(1) - 1
    n_groups, d, tg = xft_ref.shape
    te = u_ref.shape[0]
    keys_per_step = te // N_KEYS
    key0 = jnp.minimum(j, n_tiles - 1) * keys_per_step
    slab_rows_per_key = d // keys_per_step

    @pl.when(j == 0)
    def _():
        acc_ref[...] = jnp.zeros(acc_ref.shape, F32)
        coef_odd_ref[...] = jnp.zeros(coef_odd_ref.shape, BF16)

    def run(prev_ref, next_ref):
        def token_group(c, carry):
            def slab(r):
                acc_ref[c, r:r + ACC_SLAB, :] += jnp.dot(vt_ref[r:r + ACC_SLAB, :], prev_ref[c],
                                                         preferred_element_type=F32)

            for k in range(keys_per_step):
                for r in range(k * slab_rows_per_key, (k + 1) * slab_rows_per_key, ACC_SLAB):
                    slab(r)
                rows = slice(k * N_KEYS, (k + 1) * N_KEYS)
                e1_rows = [e1_ref[c, h, pl.ds(key0 + k, 1), :] for h in range(PEER_HEADS)]
                low = None
                for lg in range(tg // LANES):
                    lanes = slice(lg * LANES, (lg + 1) * LANES)
                    gate = None
                    for h in range(PEER_HEADS):
                        prod = e1_rows[h][:, lanes] * e2_ref[c, h, :, lanes]
                        sel = jnp.where(prod >= th_ref[c, h:h + 1, lanes], prod, 0.0)
                        gate = sel if gate is None else gate + sel
                    gate_ref[rows, lanes] = gate
                    low = gate if low is None else jnp.minimum(low, gate)
                if k + 1 < keys_per_step:
                    low = _fold_rows_min(low, 2 * SUBLANES)
                    prev_ref[c, 0:2 * SUBLANES, 0:LANES] += jnp.minimum(low, 0.0).astype(BF16)
            act = _gelu(jnp.dot(u_ref[...], xft_ref[c], preferred_element_type=F32))
            next_ref[c] = (gate_ref[...] * act).astype(BF16)
            return carry

        lax.fori_loop(0, n_groups, token_group, 0)

    @pl.when(j % 2 == 0)
    def _():
        run(coef_odd_ref, coef_even_ref)

    @pl.when(j % 2 == 1)
    def _():
        run(coef_even_ref, coef_odd_ref)

    @pl.when(j == n_tiles)
    def _():
        for c in range(n_groups):
            rows = slice(c * tg, (c + 1) * tg)
            x3 = x2_ref[rows, :] + acc_ref[c].T
            y_ref[rows, :] = _rms_norm(x3, gfin_ref[...]) if final_norm else x3


def _experts(xft, u_bf, vt_bf, e1, e2, th, x2, gfin, *, groups_per_tile, te, final_norm):
    t, d = x2.shape
    tg = xft.shape[2]
    gpt = groups_per_tile
    tm = gpt * tg
    n_exp = u_bf.shape[0]
    n_tiles = n_exp // te
    return pl.pallas_call(
        functools.partial(_experts_body, final_norm=final_norm),
        grid=(t // tm, n_tiles + 1),
        in_specs=[
            pl.BlockSpec((gpt, d, tg), lambda i, j: (i, 0, 0)),
            pl.BlockSpec((te, d), lambda i, j: (jnp.minimum(j, n_tiles - 1), 0)),
            pl.BlockSpec((d, te), lambda i, j: (0, jnp.maximum(j - 1, 0))),
            pl.BlockSpec((gpt, PEER_HEADS, N_KEYS, tg), lambda i, j: (i, 0, 0, 0)),
            pl.BlockSpec((gpt, PEER_HEADS, N_KEYS, tg), lambda i, j: (i, 0, 0, 0)),
            pl.BlockSpec((gpt, PEER_HEADS, tg), lambda i, j: (i, 0, 0)),
            pl.BlockSpec((tm, d), lambda i, j: (i, 0)),
            pl.BlockSpec((1, d), lambda i, j: (0, 0)),
        ],
        out_specs=pl.BlockSpec((tm, d), lambda i, j: (i, 0)),
        out_shape=jax.ShapeDtypeStruct((t, d), F32),
        scratch_shapes=[pltpu.VMEM((gpt, d, tg), F32), pltpu.VMEM((te, tg), F32),
                        pltpu.VMEM((gpt, te, tg), BF16), pltpu.VMEM((gpt, te, tg), BF16)],
        compiler_params=pltpu.CompilerParams(
            dimension_semantics=("arbitrary", "arbitrary"), vmem_limit_bytes=VMEM_LIMIT),
        name="peer_experts",
    )(xft, u_bf, vt_bf, e1, e2, th, x2, gfin.reshape(1, d))


def _pick_tile(n, pref):
    t = pref
    while n % t:
        t //= 2
    return t


def kernel(x_prompt, x_sample, state_conv, state_lru, g_mix, w_in, gmlp_ws, gmlp_bs, conv_w, conv_b,
           lru_wa, lru_ba, lru_wx, lru_bx, lru_lambda, w_out, g_ffn, peer_wq, peer_keys, peer_u,
           peer_v, g_final):
    batch, seq, d = x_prompt.shape
    dec_batch, dec_seq, _ = x_sample.shape
    depth = w_in.shape[0]
    assert batch == 1 and seq % GMLP_CHUNK == 0 and dec_seq <= CAUSAL_CHUNK
    assert d // LANES == gmlp_ws.shape[1] == lru_wa.shape[1]
    tp, ts = batch * seq, dec_batch * dec_seq

    idx = jnp.arange(GMLP_CHUNK)
    mask = (idx[:, None] // CAUSAL_CHUNK) >= (idx[None, :] // CAUSAL_CHUNK)
    hk = PEER_HEADS * N_KEYS

    xp = x_prompt.reshape(tp, d)
    xs = x_sample.reshape(ts, d)
    conv_p, lru_p, conv_s, lru_s, vrows_s = [], [], [], [], []
    for l in range(depth):
        last = l == depth - 1
        w_in_bf = w_in[l].astype(BF16)
        ws_p = (gmlp_ws[l] * mask.astype(F32)).astype(BF16)
        ws_s = ws_p[:, :dec_seq, :dec_seq]
        bias = jnp.repeat(gmlp_bs[l].T, LANES, axis=1)
        shared = (conv_w[l], conv_b[l].reshape(1, d), lru_wa[l].astype(BF16), lru_ba[l].reshape(1, d),
                  lru_wx[l].astype(BF16), lru_bx[l].reshape(1, d), lru_lambda[l].reshape(1, d),
                  w_out[l].astype(BF16))
        wqt = peer_wq[l].reshape(d, PEER_HEADS, 2, N_KEYS).transpose(2, 1, 3, 0).reshape(2 * hk, d)
        wqt = wqt.astype(BF16)
        keys_bf = peer_keys[l].astype(BF16)
        u_bf = peer_u[l].astype(BF16)
        vt_bf = peer_v[l].T.astype(BF16)
        gfin = g_final if last else jnp.ones((d,), F32)

        def ffn(x2):
            t = x2.shape[0]
            tm, tg = _pick_tile(t, 512), _pick_tile(t, 256)
            xft, e1, e2, th = _route(x2, g_ffn[l], wqt, keys_bf, tm=tg)
            return _experts(xft, u_bf, vt_bf, e1, e2, th, x2, gfin, groups_per_tile=tm // tg,
                            te=512, final_norm=last)

        zp = _inproj(xp, g_mix[l], w_in_bf, tm=_pick_tile(tp, 512), tn=1024)
        zs = _inproj(xs, g_mix[l], w_in_bf, tm=_pick_tile(ts, 512), tn=1024)
        x2p, cp, lp = _mixer(zp, xp, 0, tp // GMLP_CHUNK, GMLP_CHUNK, ws_p, bias, *shared,
                             jnp.zeros((1, CONV_W - 1, d), F32), jnp.zeros((1, 1, d), F32),
                             carried=True)
        x2s, cs, ls = _mixer(zs, xs, 0, dec_batch, dec_seq, ws_s, bias[:dec_seq], *shared,
                             state_conv[l], state_lru[l].reshape(dec_batch, 1, d), carried=False)
        xp, xs = ffn(x2p), ffn(x2s)

        conv_p.append(cp)
        lru_p.append(lp.reshape(batch, d))
        conv_s.append(cs)
        lru_s.append(ls.reshape(dec_batch, d))
        vrows_s.append(zs[:, d:2 * d].reshape(dec_batch, dec_seq, d))

    return (xp.reshape(batch, seq, d), xs.reshape(dec_batch, dec_seq, d), jnp.stack(conv_p),
            jnp.stack(lru_p), jnp.stack(conv_s), jnp.stack(lru_s), jnp.stack(vrows_s))
```

```python
import functools
import math

import jax
import jax.numpy as jnp
from jax import lax
from jax.experimental import pallas as pl
from jax.experimental.pallas import tpu as pltpu

F32 = jnp.float32
BF16 = jnp.bfloat16

NORM_EPS = 1e-6
N_PARTS = 6
LANES = 128
SUBLANES = 8
CONV_W = 4
LRU_C = 8.0
GMLP_CHUNK = 128
CAUSAL_CHUNK = 64
N_KEYS = 128
PEER_HEADS = 8
PEER_TOPK = 16
VMEM_LIMIT = 56 * 1024 * 1024
ACC_SLAB = 512
INPROJ_SLAB = 128


def _gelu(x):
    c = math.sqrt(2.0 / math.pi)
    return x * (0.5 * (1.0 + jnp.tanh(c * (x + 0.044715 * (x * x * x)))))


def _rms_norm(x, g):
    ms = jnp.mean(x * x, axis=-1, keepdims=True)
    return (x * lax.rsqrt(ms + NORM_EPS)) * g


def _inproj_body(x_ref, g_ref, w_ref, z_ref, xn_ref, *, blocks_per_part):
    j = pl.program_id(1)

    @pl.when(j == 0)
    def _():
        xn_ref[...] = _rms_norm(x_ref[...], g_ref[...]).astype(BF16)

    def project(activation):
        for r in range(0, z_ref.shape[0], INPROJ_SLAB):
            rows = slice(r, r + INPROJ_SLAB)
            z_ref[rows, :] = activation(
                jnp.dot(xn_ref[rows, :], w_ref[...], preferred_element_type=F32))

    part = j // blocks_per_part
    is_gelu = jnp.logical_or(part <= 1, part == 3)

    @pl.when(is_gelu)
    def _():
        project(_gelu)

    @pl.when(part == 2)
    def _():
        project(lambda z: z)

    @pl.when(part >= 4)
    def _():
        project(jax.nn.sigmoid)


def _inproj(x, g, w_bf, *, tm, tn):
    t, d = x.shape
    n = w_bf.shape[1]
    return pl.pallas_call(
        functools.partial(_inproj_body, blocks_per_part=(n // N_PARTS) // tn),
        grid=(t // tm, n // tn),
        in_specs=[
            pl.BlockSpec((tm, d), lambda i, j: (i, 0)),
            pl.BlockSpec((1, d), lambda i, j: (0, 0)),
            pl.BlockSpec((d, tn), lambda i, j: (0, j)),
        ],
        out_specs=pl.BlockSpec((tm, tn), lambda i, j: (i, j)),
        out_shape=jax.ShapeDtypeStruct((t, n), F32),
        scratch_shapes=[pltpu.VMEM((tm, d), BF16)],
        compiler_params=pltpu.CompilerParams(
            dimension_semantics=("arbitrary", "arbitrary"), vmem_limit_bytes=VMEM_LIMIT),
        name="inproj",
    )(x, g.reshape(1, d), w_bf)


def _scan_rows(a, b, h_in):
    sub = lax.broadcasted_iota(jnp.int32, (SUBLANES, a.shape[1]), 0)
    keep = {s: sub >= s for s in (1, 2, 4)}
    blocks = []
    carry = h_in
    for r in range(0, a.shape[0], SUBLANES):
        ab, bb = a[r:r + SUBLANES, :], b[r:r + SUBLANES, :]
        for s in (1, 2, 4):
            a_sh = jnp.where(keep[s], pltpu.roll(ab, s, axis=0), 1.0)
            b_sh = jnp.where(keep[s], pltpu.roll(bb, s, axis=0), 0.0)
            bb = ab * b_sh + bb
            ab = ab * a_sh
        hb = ab * carry + bb
        carry = hb[SUBLANES - 1:SUBLANES, :]
        blocks.append(hb)
    return jnp.concatenate(blocks, axis=0), carry


def _mixer_body(zu_ref, zv_ref, zx_ref, zg_ref, zga_ref, zgb_ref, x_ref,
                wg_ref, bg_ref, cw_ref, cb_ref, wa_ref, ba_ref, wx_ref, bx_ref, lam_ref,
                wout_ref, prev_ref, h0_ref,
                x2_ref, conv_ref, hlast_ref,
                xp_ref, merged_ref, hc_ref, *, carried):
    c = pl.program_id(0)
    rows_c = zx_ref.shape[0]
    pad = SUBLANES

    if carried:
        @pl.when(c == 0)
        def _():
            xp_ref[0:pad, :] = jnp.zeros((pad, xp_ref.shape[1]), F32)
            hc_ref[...] = jnp.zeros(hc_ref.shape, F32)
    else:
        xp_ref[pad - (CONV_W - 1):pad, :] = prev_ref[0]
        hc_ref[0:1, :] = h0_ref[0]
    xp_ref[pad:pad + rows_c, :] = zx_ref[...]

    def group(j, carry):
        col = pl.ds(pl.multiple_of(j * LANES, LANES), LANES)
        s = jnp.dot(wg_ref[j], zv_ref[:, col].astype(BF16), preferred_element_type=F32) + bg_ref[:, col]
        y_a = zu_ref[:, col] * s
        xc = cb_ref[:, col]
        for k in range(CONV_W):
            lo = pad - (CONV_W - 1) + k
            xc = xc + xp_ref[lo:lo + rows_c, col] * cw_ref[k:k + 1, col]
        xcb = xc.astype(BF16)
        r = jax.nn.sigmoid(jnp.dot(xcb, wa_ref[j], preferred_element_type=F32) + ba_ref[:, col])
        i = jax.nn.sigmoid(jnp.dot(xcb, wx_ref[j], preferred_element_type=F32) + bx_ref[:, col])
        nlam = -lam_ref[:, col]
        softplus = jnp.maximum(nlam, 0.0) + jnp.log1p(jnp.exp(-jnp.abs(nlam)))
        log_a = (-LRU_C) * r * softplus
        a = jnp.exp(log_a)
        mult = jnp.sqrt(jnp.tanh(-log_a) * (a * a + 1.0))
        if carried:
            rows = lax.broadcasted_iota(jnp.int32, mult.shape, 0)
            mult = jnp.where(jnp.logical_and(rows == 0, c == 0), 1.0, mult)
        b = mult * i * xc
        h, h_last = _scan_rows(a, b, hc_ref[0:1, col])
        hc_ref[0:1, col] = h_last
        y_b = h * zg_ref[:, col]
        merged_ref[:, col] = (zga_ref[:, col] * y_a + zgb_ref[:, col] * y_b).astype(BF16)
        return carry

    lax.fori_loop(0, zx_ref.shape[1] // LANES, group, 0, unroll=2)

    x2_ref[...] = x_ref[...] + jnp.dot(merged_ref[...], wout_ref[...], preferred_element_type=F32)
    conv_ref[0] = xp_ref[pad + rows_c - (CONV_W - 1):pad + rows_c, :]
    hlast_ref[0] = hc_ref[0:1, :]
    if carried:
        xp_ref[0:pad, :] = xp_ref[rows_c:rows_c + pad, :]


def _mixer(z, x, row0, n_seq, rows_c, wg, bg, cw, cb, wa, ba, wx, bx, lam, wout, prev, h0, *, carried):
    d = x.shape[1]
    blk0 = row0 // rows_c
    n_state = prev.shape[0]

    def zspec(part):
        return pl.BlockSpec((rows_c, d), lambda c, part=part: (blk0 + c, part))

    def const(shape):
        return pl.BlockSpec(shape, lambda c: (0,) * len(shape))

    def state_idx(c):
        return (c, 0, 0) if not carried else (0, 0, 0)

    return pl.pallas_call(
        functools.partial(_mixer_body, carried=carried),
        grid=(n_seq,),
        in_specs=[zspec(p) for p in range(N_PARTS)] + [
            pl.BlockSpec((rows_c, d), lambda c: (blk0 + c, 0)),
            const(wg.shape), const(bg.shape), const(cw.shape), const(cb.shape),
            const(wa.shape), const(ba.shape), const(wx.shape), const(bx.shape), const(lam.shape),
            const(wout.shape),
            pl.BlockSpec((1, CONV_W - 1, d), state_idx),
            pl.BlockSpec((1, 1, d), state_idx),
        ],
        out_specs=[
            pl.BlockSpec((rows_c, d), lambda c: (c, 0)),
            pl.BlockSpec((1, CONV_W - 1, d), state_idx),
            pl.BlockSpec((1, 1, d), state_idx),
        ],
        out_shape=[
            jax.ShapeDtypeStruct((n_seq * rows_c, d), F32),
            jax.ShapeDtypeStruct((n_state, CONV_W - 1, d), F32),
            jax.ShapeDtypeStruct((n_state, 1, d), F32),
        ],
        scratch_shapes=[
            pltpu.VMEM((rows_c + 2 * SUBLANES, d), F32),
            pltpu.VMEM((rows_c, d), BF16),
            pltpu.VMEM((SUBLANES, d), F32),
        ],
        compiler_params=pltpu.CompilerParams(
            dimension_semantics=("arbitrary",), vmem_limit_bytes=VMEM_LIMIT),
        name="mixer_stream" if carried else "mixer_step",
    )(z, z, z, z, z, z, x, wg, bg, cw, cb, wa, ba, wx, bx, lam, wout, prev, h0)


def _sort_network(n):
    pairs = []
    p = 1
    while p < n:
        k = p
        while k >= 1:
            for j in range(k % p, n - k, 2 * k):
                for i in range(min(k, n - j - k)):
                    if (i + j) // (2 * p) == (i + j + k) // (2 * p):
                        pairs.append((i + j, i + j + k))
            k //= 2
        p *= 2
    return pairs


_SORT16 = _sort_network(PEER_TOPK)


def _sort_desc(v):
    v = list(v)
    for i, j in _SORT16:
        v[i], v[j] = jnp.maximum(v[i], v[j]), jnp.minimum(v[i], v[j])
    return v


def _merge_top(xs, ys):
    z = [jnp.maximum(xs[i], ys[PEER_TOPK - 1 - i]) for i in range(PEER_TOPK)]
    d = PEER_TOPK // 2
    while d >= 1:
        for i in range(PEER_TOPK):
            if (i & d) == 0:
                z[i], z[i + d] = jnp.maximum(z[i], z[i + d]), jnp.minimum(z[i], z[i + d])
        d //= 2
    return z


def _top_of_lists(lists):
    while len(lists) > 1:
        lists = [_merge_top(lists[i], lists[i + 1]) for i in range(0, len(lists), 2)]
    return lists[0]


def _route_body(x2_ref, g_ref, wqt_ref, keys_ref, xft_ref, e1_ref, e2_ref, th_ref, s_ref):
    tm = x2_ref.shape[0]
    hk = PEER_HEADS * N_KEYS
    xft = _rms_norm(x2_ref[...], g_ref[...]).T.astype(BF16)
    xft_ref[0] = xft
    qt = jnp.dot(wqt_ref[...], xft, preferred_element_type=F32)
    for p in range(2):
        for h in range(PEER_HEADS):
            lo = p * hk + h * N_KEYS
            q_hp = qt[lo:lo + N_KEYS, :].astype(BF16)
            s_hp = jnp.dot(keys_ref[p], q_hp, preferred_element_type=F32)
            for g in range(tm // LANES):
                s_ref[p, g, h * N_KEYS:(h + 1) * N_KEYS, :] = s_hp[:, g * LANES:(g + 1) * LANES]

    def lane_group(lg, carry):
        lanes = pl.ds(pl.multiple_of(lg * LANES, LANES), LANES)
        tops = []
        for p in range(2):
            vals = [s_ref[p, lg, pl.ds(n, PEER_HEADS, stride=N_KEYS), :] for n in range(N_KEYS)]
            groups = [_sort_desc(vals[i:i + PEER_TOPK]) for i in range(0, N_KEYS, PEER_TOPK)]
            tops.append(_top_of_lists(groups))
        m1, m2 = tops[0][0], tops[1][0]
        e1 = [jnp.exp(v - m1) for v in tops[0]]
        e2 = [jnp.exp(v - m2) for v in tops[1]]
        best = _top_of_lists([[e1[i] * e2[j] for j in range(PEER_TOPK)] for i in range(PEER_TOPK)])
        z = best[0]
        for v in best[1:]:
            z = z + v
        rz = 1.0 / z
        e1n = [v * rz for v in e1]
        bestn = _top_of_lists([[e1n[i] * e2[j] for j in range(PEER_TOPK)] for i in range(PEER_TOPK)])
        th_ref[0, :, lanes] = bestn[PEER_TOPK - 1]
        for h in range(PEER_HEADS):
            rows = slice(h * N_KEYS, (h + 1) * N_KEYS)
            e1_ref[0, h, :, lanes] = jnp.exp(s_ref[0, lg, rows, :] - m1[h:h + 1, :]) * rz[h:h + 1, :]
            e2_ref[0, h, :, lanes] = jnp.exp(s_ref[1, lg, rows, :] - m2[h:h + 1, :])
        return carry

    lax.fori_loop(0, tm // LANES, lane_group, 0)


def _route(x2, g, wqt, keys_bf, *, tm):
    t, d = x2.shape
    hk = PEER_HEADS * N_KEYS
    n_groups = t // tm
    return pl.pallas_call(
        _route_body,
        grid=(t // tm,),
        in_specs=[
            pl.BlockSpec((tm, d), lambda i: (i, 0)),
            pl.BlockSpec((1, d), lambda i: (0, 0)),
            pl.BlockSpec(wqt.shape, lambda i: (0, 0)),
            pl.BlockSpec(keys_bf.shape, lambda i: (0, 0, 0)),
        ],
        out_specs=[
            pl.BlockSpec((1, d, tm), lambda i: (i, 0, 0)),
            pl.BlockSpec((1, PEER_HEADS, N_KEYS, tm), lambda i: (i, 0, 0, 0)),
            pl.BlockSpec((1, PEER_HEADS, N_KEYS, tm), lambda i: (i, 0, 0, 0)),
            pl.BlockSpec((1, PEER_HEADS, tm), lambda i: (i, 0, 0)),
        ],
        out_shape=[
            jax.ShapeDtypeStruct((n_groups, d, tm), BF16),
            jax.ShapeDtypeStruct((n_groups, PEER_HEADS, N_KEYS, tm), F32),
            jax.ShapeDtypeStruct((n_groups, PEER_HEADS, N_KEYS, tm), F32),
            jax.ShapeDtypeStruct((n_groups, PEER_HEADS, tm), F32),
        ],
        scratch_shapes=[pltpu.VMEM((2, tm // LANES, hk, LANES), F32)],
        compiler_params=pltpu.CompilerParams(
            dimension_semantics=("arbitrary",), vmem_limit_bytes=VMEM_LIMIT),
        name="peer_route",
    )(x2, g.reshape(1, d), wqt, keys_bf)


def _fold_rows_min(x, rows):
    while x.shape[0] > rows:
        half = x.shape[0] // 2
        x = jnp.minimum(x[:half], x[half:])
    return x


def _experts_body(xft_ref, u_ref, vt_ref, e1_ref, e2_ref, th_ref, x2_ref, gfin_ref,
                  y_ref, acc_ref, gate_ref, coef_even_ref, coef_odd_ref, *, final_norm):
    j = pl.program_id(1)
    n_tiles = pl.num_programs(1) - 1
    n_groups, d, tg = xft_ref.shape
    te = u_ref.shape[0]
    keys_per_step = te // N_KEYS
    key0 = jnp.minimum(j, n_tiles - 1) * keys_per_step
    slab_rows_per_key = d // keys_per_step

    @pl.when(j == 0)
    def _():
        acc_ref[...] = jnp.zeros(acc_ref.shape, F32)
        coef_odd_ref[...] = jnp.zeros(coef_odd_ref.shape, BF16)

    def run(prev_ref, next_ref):
        def token_group(c, carry):
            def slab(r):
                acc_ref[c, r:r + ACC_SLAB, :] += jnp.dot(vt_ref[r:r + ACC_SLAB, :], prev_ref[c],
                                                         preferred_element_type=F32)

            for k in range(keys_per_step):
                for r in range(k * slab_rows_per_key, (k + 1) * slab_rows_per_key, ACC_SLAB):
                    slab(r)
                rows = slice(k * N_KEYS, (k + 1) * N_KEYS)
                e1_rows = [e1_ref[c, h, pl.ds(key0 + k, 1), :] for h in range(PEER_HEADS)]
                low = None
                for lg in range(tg // LANES):
                    lanes = slice(lg * LANES, (lg + 1) * LANES)
                    gate = None
                    for h in range(PEER_HEADS):
                        prod = e1_rows[h][:, lanes] * e2_ref[c, h, :, lanes]
                        sel = jnp.where(prod >= th_ref[c, h:h + 1, lanes], prod, 0.0)
                        gate = sel if gate is None else gate + sel
                    gate_ref[rows, lanes] = gate
                    low = gate if low is None else jnp.minimum(low, gate)
                if k + 1 < keys_per_step:
                    low = _fold_rows_min(low, 2 * SUBLANES)
                    prev_ref[c, 0:2 * SUBLANES, 0:LANES] += jnp.minimum(low, 0.0).astype(BF16)
            act = _gelu(jnp.dot(u_ref[...], xft_ref[c], preferred_element_type=F32))
            next_ref[c] = (gate_ref[...] * act).astype(BF16)
            return carry

        lax.fori_loop(0, n_groups, token_group, 0)

    @pl.when(j % 2 == 0)
    def _():
        run(coef_odd_ref, coef_even_ref)

    @pl.when(j % 2 == 1)
    def _():
        run(coef_even_ref, coef_odd_ref)

    @pl.when(j == n_tiles)
    def _():
        for c in range(n_groups):
            rows = slice(c * tg, (c + 1) * tg)
            x3 = x2_ref[rows, :] + acc_ref[c].T
            y_ref[rows, :] = _rms_norm(x3, gfin_ref[...]) if final_norm else x3


def _experts(xft, u_bf, vt_bf, e1, e2, th, x2, gfin, *, groups_per_tile, te, final_norm):
    t, d = x2.shape
    tg = xft.shape[2]
    gpt = groups_per_tile
    tm = gpt * tg
    n_exp = u_bf.shape[0]
    n_tiles = n_exp // te
    return pl.pallas_call(
        functools.partial(_experts_body, final_norm=final_norm),
        grid=(t // tm, n_tiles + 1),
        in_specs=[
            pl.BlockSpec((gpt, d, tg), lambda i, j: (i, 0, 0)),
            pl.BlockSpec((te, d), lambda i, j: (jnp.minimum(j, n_tiles - 1), 0)),
            pl.BlockSpec((d, te), lambda i, j: (0, jnp.maximum(j - 1, 0))),
            pl.BlockSpec((gpt, PEER_HEADS, N_KEYS, tg), lambda i, j: (i, 0, 0, 0)),
            pl.BlockSpec((gpt, PEER_HEADS, N_KEYS, tg), lambda i, j: (i, 0, 0, 0)),
            pl.BlockSpec((gpt, PEER_HEADS, tg), lambda i, j: (i, 0, 0)),
            pl.BlockSpec((tm, d), lambda i, j: (i, 0)),
            pl.BlockSpec((1, d), lambda i, j: (0, 0)),
        ],
        out_specs=pl.BlockSpec((tm, d), lambda i, j: (i, 0)),
        out_shape=jax.ShapeDtypeStruct((t, d), F32),
        scratch_shapes=[pltpu.VMEM((gpt, d, tg), F32), pltpu.VMEM((te, tg), F32),
                        pltpu.VMEM((gpt, te, tg), BF16), pltpu.VMEM((gpt, te, tg), BF16)],
        compiler_params=pltpu.CompilerParams(
            dimension_semantics=("arbitrary", "arbitrary"), vmem_limit_bytes=VMEM_LIMIT),
        name="peer_experts",
    )(xft, u_bf, vt_bf, e1, e2, th, x2, gfin.reshape(1, d))


def _pick_tile(n, pref):
    t = pref
    while n % t:
        t //= 2
    return t


def kernel(x_prompt, x_sample, state_conv, state_lru, g_mix, w_in, gmlp_ws, gmlp_bs, conv_w, conv_b,
           lru_wa, lru_ba, lru_wx, lru_bx, lru_lambda, w_out, g_ffn, peer_wq, peer_keys, peer_u,
           peer_v, g_final):
    batch, seq, d = x_prompt.shape
    dec_batch, dec_seq, _ = x_sample.shape
    depth = w_in.shape[0]
    assert batch == 1 and seq % GMLP_CHUNK == 0 and dec_seq <= CAUSAL_CHUNK
    assert d // LANES == gmlp_ws.shape[1] == lru_wa.shape[1]
    tp, ts = batch * seq, dec_batch * dec_seq

    idx = jnp.arange(GMLP_CHUNK)
    mask = (idx[:, None] // CAUSAL_CHUNK) >= (idx[None, :] // CAUSAL_CHUNK)
    hk = PEER_HEADS * N_KEYS

    xp = x_prompt.reshape(tp, d)
    xs = x_sample.reshape(ts, d)
    conv_p, lru_p, conv_s, lru_s, vrows_s = [], [], [], [], []
    for l in range(depth):
        last = l == depth - 1
        w_in_bf = w_in[l].astype(BF16)
        ws_p = (gmlp_ws[l] * mask.astype(F32)).astype(BF16)
        ws_s = ws_p[:, :dec_seq, :dec_seq]
        bias = jnp.repeat(gmlp_bs[l].T, LANES, axis=1)
        shared = (conv_w[l], conv_b[l].reshape(1, d), lru_wa[l].astype(BF16), lru_ba[l].reshape(1, d),
                  lru_wx[l].astype(BF16), lru_bx[l].reshape(1, d), lru_lambda[l].reshape(1, d),
                  w_out[l].astype(BF16))
        wqt = peer_wq[l].reshape(d, PEER_HEADS, 2, N_KEYS).transpose(2, 1, 3, 0).reshape(2 * hk, d)
        wqt = wqt.astype(BF16)
        keys_bf = peer_keys[l].astype(BF16)
        u_bf = peer_u[l].astype(BF16)
        vt_bf = peer_v[l].T.astype(BF16)
        gfin = g_final if last else jnp.ones((d,), F32)

        def ffn(x2):
            t = x2.shape[0]
            tm, tg = _pick_tile(t, 512), _pick_tile(t, 256)
            xft, e1, e2, th = _route(x2, g_ffn[l], wqt, keys_bf, tm=tg)
            return _experts(xft, u_bf, vt_bf, e1, e2, th, x2, gfin, groups_per_tile=tm // tg,
                            te=512, final_norm=last)

        zp = _inproj(xp, g_mix[l], w_in_bf, tm=_pick_tile(tp, 512), tn=1024)
        zs = _inproj(xs, g_mix[l], w_in_bf, tm=_pick_tile(ts, 512), tn=1024)
        x2p, cp, lp = _mixer(zp, xp, 0, tp // GMLP_CHUNK, GMLP_CHUNK, ws_p, bias, *shared,
                             jnp.zeros((1, CONV_W - 1, d), F32), jnp.zeros((1, 1, d), F32),
                             carried=True)
        x2s, cs, ls = _mixer(zs, xs, 0, dec_batch, dec_seq, ws_s, bias[:dec_seq], *shared,
                             state_conv[l], state_lru[l].reshape(dec_batch, 1, d), carried=False)
        xp, xs = ffn(x2p), ffn(x2s)

        conv_p.append(cp)
        lru_p.append(lp.reshape(batch, d))
        conv_s.append(cs)
        lru_s.append(ls.reshape(dec_batch, d))
        vrows_s.append(zs[:, d:2 * d].reshape(dec_batch, dec_seq, d))

    return (xp.reshape(batch, seq, d), xs.reshape(dec_batch, dec_seq, d), jnp.stack(conv_p),
            jnp.stack(lru_p), jnp.stack(conv_s), jnp.stack(lru_s), jnp.stack(vrows_s))
```

```python
import functools
import math

import jax
import jax.numpy as jnp
from jax import lax
from jax.experimental import pallas as pl
from jax.experimental.pallas import tpu as pltpu

F32 = jnp.float32
BF16 = jnp.bfloat16

NORM_EPS = 1e-6
N_PARTS = 6
LANES = 128
SUBLANES = 8
CONV_W = 4
LRU_C = 8.0
GMLP_CHUNK = 128
CAUSAL_CHUNK = 64
N_KEYS = 128
PEER_HEADS = 8
PEER_TOPK = 16
VMEM_LIMIT = 56 * 1024 * 1024
ACC_SLAB = 512
INPROJ_SLAB = 128


def _gelu(x):
    c = math.sqrt(2.0 / math.pi)
    return x * (0.5 * (1.0 + jnp.tanh(c * (x + 0.044715 * (x * x * x)))))


def _rms_norm(x, g):
    ms = jnp.mean(x * x, axis=-1, keepdims=True)
    return (x * lax.rsqrt(ms + NORM_EPS)) * g


def _prenorm_body(x_ref, g_ref, xn_ref):
    xn_ref[...] = _rms_norm(x_ref[...], g_ref[...]).astype(BF16)


def _prenorm(x, g, *, tm):
    t, d = x.shape
    return pl.pallas_call(
        _prenorm_body,
        grid=(t // tm,),
        in_specs=[pl.BlockSpec((tm, d), lambda i: (i, 0)), pl.BlockSpec((1, d), lambda i: (0, 0))],
        out_specs=pl.BlockSpec((tm, d), lambda i: (i, 0)),
        out_shape=jax.ShapeDtypeStruct((t, d), BF16),
        compiler_params=pltpu.CompilerParams(
            dimension_semantics=("arbitrary",), vmem_limit_bytes=VMEM_LIMIT),
        name="prenorm",
    )(x, g.reshape(1, d))


def _inproj_body(xn_ref, w_ref, z_ref):
    part = pl.program_id(0)

    def project(activation):
        for r in range(0, z_ref.shape[0], INPROJ_SLAB):
            rows = slice(r, r + INPROJ_SLAB)
            z_ref[rows, :] = activation(
                jnp.dot(xn_ref[rows, :], w_ref[...], preferred_element_type=F32))

    is_gelu = jnp.logical_or(part <= 1, part == 3)

    @pl.when(is_gelu)
    def _():
        project(_gelu)

    @pl.when(part == 2)
    def _():
        project(lambda z: z)

    @pl.when(part >= 4)
    def _():
        project(jax.nn.sigmoid)


def _inproj(xn, w_bf, *, tm):
    t, d = xn.shape
    n = w_bf.shape[1]
    tn = n // N_PARTS
    return pl.pallas_call(
        _inproj_body,
        grid=(N_PARTS, t // tm),
        in_specs=[
            pl.BlockSpec((tm, d), lambda p, i: (i, 0)),
            pl.BlockSpec((d, tn), lambda p, i: (0, p)),
        ],
        out_specs=pl.BlockSpec((tm, tn), lambda p, i: (i, p)),
        out_shape=jax.ShapeDtypeStruct((t, n), F32),
        compiler_params=pltpu.CompilerParams(
            dimension_semantics=("arbitrary", "arbitrary"), vmem_limit_bytes=VMEM_LIMIT),
        name="inproj",
    )(xn, w_bf)


def _scan_rows(a, b, h_in):
    sub = lax.broadcasted_iota(jnp.int32, (SUBLANES, a.shape[1]), 0)
    keep = {s: sub >= s for s in (1, 2, 4)}
    blocks = []
    carry = h_in
    for r in range(0, a.shape[0], SUBLANES):
        ab, bb = a[r:r + SUBLANES, :], b[r:r + SUBLANES, :]
        for s in (1, 2, 4):
            a_sh = jnp.where(keep[s], pltpu.roll(ab, s, axis=0), 1.0)
            b_sh = jnp.where(keep[s], pltpu.roll(bb, s, axis=0), 0.0)
            bb = ab * b_sh + bb
            ab = ab * a_sh
        hb = ab * carry + bb
        carry = hb[SUBLANES - 1:SUBLANES, :]
        blocks.append(hb)
    return jnp.concatenate(blocks, axis=0), carry


def _mixer_body(zu_ref, zv_ref, zx_ref, zg_ref, zga_ref, zgb_ref, x_ref,
                wg_ref, bg_ref, cw_ref, cb_ref, wa_ref, ba_ref, wx_ref, bx_ref, lam_ref,
                wout_ref, prev_ref, h0_ref,
                x2_ref, conv_ref, hlast_ref,
                xp_ref, merged_ref, hc_ref, *, carried):
    c = pl.program_id(0)
    rows_c = zx_ref.shape[0]
    pad = SUBLANES

    if carried:
        @pl.when(c == 0)
        def _():
            xp_ref[0:pad, :] = jnp.zeros((pad, xp_ref.shape[1]), F32)
            hc_ref[...] = jnp.zeros(hc_ref.shape, F32)
    else:
        xp_ref[pad - (CONV_W - 1):pad, :] = prev_ref[0]
        hc_ref[0:1, :] = h0_ref[0]
    xp_ref[pad:pad + rows_c, :] = zx_ref[...]

    def group(j, carry):
        col = pl.ds(pl.multiple_of(j * LANES, LANES), LANES)
        s = jnp.dot(wg_ref[j], zv_ref[:, col].astype(BF16), preferred_element_type=F32) + bg_ref[:, col]
        y_a = zu_ref[:, col] * s
        xc = cb_ref[:, col]
        for k in range(CONV_W):
            lo = pad - (CONV_W - 1) + k
            xc = xc + xp_ref[lo:lo + rows_c, col] * cw_ref[k:k + 1, col]
        xcb = xc.astype(BF16)
        r = jax.nn.sigmoid(jnp.dot(xcb, wa_ref[j], preferred_element_type=F32) + ba_ref[:, col])
        i = jax.nn.sigmoid(jnp.dot(xcb, wx_ref[j], preferred_element_type=F32) + bx_ref[:, col])
        nlam = -lam_ref[:, col]
        softplus = jnp.maximum(nlam, 0.0) + jnp.log1p(jnp.exp(-jnp.abs(nlam)))
        log_a = (-LRU_C) * r * softplus
        a = jnp.exp(log_a)
        mult = jnp.sqrt(jnp.tanh(-log_a) * (a * a + 1.0))
        if carried:
            rows = lax.broadcasted_iota(jnp.int32, mult.shape, 0)
            mult = jnp.where(jnp.logical_and(rows == 0, c == 0), 1.0, mult)
        b = mult * i * xc
        h, h_last = _scan_rows(a, b, hc_ref[0:1, col])
        hc_ref[0:1, col] = h_last
        y_b = h * zg_ref[:, col]
        merged_ref[:, col] = (zga_ref[:, col] * y_a + zgb_ref[:, col] * y_b).astype(BF16)
        return carry

    lax.fori_loop(0, zx_ref.shape[1] // LANES, group, 0, unroll=2)

    x2_ref[...] = x_ref[...] + jnp.dot(merged_ref[...], wout_ref[...], preferred_element_type=F32)
    conv_ref[0] = xp_ref[pad + rows_c - (CONV_W - 1):pad + rows_c, :]
    hlast_ref[0] = hc_ref[0:1, :]
    if carried:
        xp_ref[0:pad, :] = xp_ref[rows_c:rows_c + pad, :]


def _mixer(z, x, row0, n_seq, rows_c, wg, bg, cw, cb, wa, ba, wx, bx, lam, wout, prev, h0, *, carried):
    d = x.shape[1]
    blk0 = row0 // rows_c
    n_state = prev.shape[0]

    def zspec(part):
        return pl.BlockSpec((rows_c, d), lambda c, part=part: (blk0 + c, part))

    def const(shape):
        return pl.BlockSpec(shape, lambda c: (0,) * len(shape))

    def state_idx(c):
        return (c, 0, 0) if not carried else (0, 0, 0)

    return pl.pallas_call(
        functools.partial(_mixer_body, carried=carried),
        grid=(n_seq,),
        in_specs=[zspec(p) for p in range(N_PARTS)] + [
            pl.BlockSpec((rows_c, d), lambda c: (blk0 + c, 0)),
            const(wg.shape), const(bg.shape), const(cw.shape), const(cb.shape),
            const(wa.shape), const(ba.shape), const(wx.shape), const(bx.shape), const(lam.shape),
            const(wout.shape),
            pl.BlockSpec((1, CONV_W - 1, d), state_idx),
            pl.BlockSpec((1, 1, d), state_idx),
        ],
        out_specs=[
            pl.BlockSpec((rows_c, d), lambda c: (c, 0)),
            pl.BlockSpec((1, CONV_W - 1, d), state_idx),
            pl.BlockSpec((1, 1, d), state_idx),
        ],
        out_shape=[
            jax.ShapeDtypeStruct((n_seq * rows_c, d), F32),
            jax.ShapeDtypeStruct((n_state, CONV_W - 1, d), F32),
            jax.ShapeDtypeStruct((n_state, 1, d), F32),
        ],
        scratch_shapes=[
            pltpu.VMEM((rows_c + 2 * SUBLANES, d), F32),
            pltpu.VMEM((rows_c, d), BF16),
            pltpu.VMEM((SUBLANES, d), F32),
        ],
        compiler_params=pltpu.CompilerParams(
            dimension_semantics=("arbitrary",), vmem_limit_bytes=VMEM_LIMIT),
        name="mixer_stream" if carried else "mixer_step",
    )(z, z, z, z, z, z, x, wg, bg, cw, cb, wa, ba, wx, bx, lam, wout, prev, h0)


def _sort_network(n):
    pairs = []
    p = 1
    while p < n:
        k = p
        while k >= 1:
            for j in range(k % p, n - k, 2 * k):
                for i in range(min(k, n - j - k)):
                    if (i + j) // (2 * p) == (i + j + k) // (2 * p):
                        pairs.append((i + j, i + j + k))
            k //= 2
        p *= 2
    return pairs


_SORT16 = _sort_network(PEER_TOPK)


def _sort_desc(v):
    v = list(v)
    for i, j in _SORT16:
        v[i], v[j] = jnp.maximum(v[i], v[j]), jnp.minimum(v[i], v[j])
    return v


def _merge_top(xs, ys):
    z = [jnp.maximum(xs[i], ys[PEER_TOPK - 1 - i]) for i in range(PEER_TOPK)]
    d = PEER_TOPK // 2
    while d >= 1:
        for i in range(PEER_TOPK):
            if (i & d) == 0:
                z[i], z[i + d] = jnp.maximum(z[i], z[i + d]), jnp.minimum(z[i], z[i + d])
        d //= 2
    return z


def _top_of_lists(lists):
    while len(lists) > 1:
        lists = [_merge_top(lists[i], lists[i + 1]) for i in range(0, len(lists), 2)]
    return lists[0]


def _round_bf16(x):
    return x.astype(BF16).astype(F32)


def _route_body(x2_ref, g_ref, wqt_ref, keys_ref, xft_ref, e1_ref, e2_ref, th_ref, s_ref):
    tm = x2_ref.shape[0]
    hk = PEER_HEADS * N_KEYS
    xft = _rms_norm(x2_ref[...], g_ref[...]).T.astype(BF16)
    xft_ref[0] = xft
    qt = jnp.dot(wqt_ref[...], xft, preferred_element_type=F32)
    for p in range(2):
        for h in range(PEER_HEADS):
            lo = p * hk + h * N_KEYS
            q_hp = qt[lo:lo + N_KEYS, :].astype(BF16)
            s_hp = jnp.dot(keys_ref[p], q_hp, preferred_element_type=F32)
            for g in range(tm // LANES):
                s_ref[p, g, h * N_KEYS:(h + 1) * N_KEYS, :] = s_hp[:, g * LANES:(g + 1) * LANES]

    def lane_group(lg, carry):
        lanes = pl.ds(pl.multiple_of(lg * LANES, LANES), LANES)
        tops = []
        for p in range(2):
            vals = [s_ref[p, lg, pl.ds(n, PEER_HEADS, stride=N_KEYS), :] for n in range(N_KEYS)]
            groups = [_sort_desc(vals[i:i + PEER_TOPK]) for i in range(0, N_KEYS, PEER_TOPK)]
            tops.append(_top_of_lists(groups))
        m1, m2 = tops[0][0], tops[1][0]
        e1 = [jnp.exp(v - m1) for v in tops[0]]
        e2 = [jnp.exp(v - m2) for v in tops[1]]
        best = _top_of_lists([[e1[i] * e2[j] for j in range(PEER_TOPK)] for i in range(PEER_TOPK)])
        z = best[0]
        for v in best[1:]:
            z = z + v
        rz = 1.0 / z
        e1n = [_round_bf16(v * rz) for v in e1]
        e2n = [_round_bf16(v) for v in e2]
        bestn = _top_of_lists([[_round_bf16(e1n[i] * e2n[j]) for j in range(PEER_TOPK)]
                               for i in range(PEER_TOPK)])
        th_ref[0, :, lanes] = bestn[PEER_TOPK - 1]
        for h in range(PEER_HEADS):
            rows = slice(h * N_KEYS, (h + 1) * N_KEYS)
            e1_ref[0, h, :, lanes] = jnp.exp(s_ref[0, lg, rows, :] - m1[h:h + 1, :]) * rz[h:h + 1, :]
            e2_ref[0, h, :, lanes] = jnp.exp(s_ref[1, lg, rows, :] - m2[h:h + 1, :]).astype(BF16)
        return carry

    lax.fori_loop(0, tm // LANES, lane_group, 0)


def _route(x2, g, wqt, keys_bf, *, tm):
    t, d = x2.shape
    hk = PEER_HEADS * N_KEYS
    n_groups = t // tm
    return pl.pallas_call(
        _route_body,
        grid=(t // tm,),
        in_specs=[
            pl.BlockSpec((tm, d), lambda i: (i, 0)),
            pl.BlockSpec((1, d), lambda i: (0, 0)),
            pl.BlockSpec(wqt.shape, lambda i: (0, 0)),
            pl.BlockSpec(keys_bf.shape, lambda i: (0, 0, 0)),
        ],
        out_specs=[
            pl.BlockSpec((1, d, tm), lambda i: (i, 0, 0)),
            pl.BlockSpec((1, PEER_HEADS, N_KEYS, tm), lambda i: (i, 0, 0, 0)),
            pl.BlockSpec((1, PEER_HEADS, N_KEYS, tm), lambda i: (i, 0, 0, 0)),
            pl.BlockSpec((1, PEER_HEADS, tm), lambda i: (i, 0, 0)),
        ],
        out_shape=[
            jax.ShapeDtypeStruct((n_groups, d, tm), BF16),
            jax.ShapeDtypeStruct((n_groups, PEER_HEADS, N_KEYS, tm), F32),
            jax.ShapeDtypeStruct((n_groups, PEER_HEADS, N_KEYS, tm), BF16),
            jax.ShapeDtypeStruct((n_groups, PEER_HEADS, tm), F32),
        ],
        scratch_shapes=[pltpu.VMEM((2, tm // LANES, hk, LANES), F32)],
        compiler_params=pltpu.CompilerParams(
            dimension_semantics=("arbitrary",), vmem_limit_bytes=VMEM_LIMIT),
        name="peer_route",
    )(x2, g.reshape(1, d), wqt, keys_bf)


def _rows_bf16(row):
    tile = jnp.broadcast_to(row, (2 * SUBLANES, row.shape[1])).astype(BF16)
    return pltpu.repeat(tile, N_KEYS // (2 * SUBLANES), axis=0)


def _fold_rows_min(x, rows):
    while x.shape[0] > rows:
        half = x.shape[0] // 2
        x = jnp.minimum(x[:half], x[half:])
    return x


def _experts_body(xft_ref, u_ref, vt_ref, e1_ref, e2_ref, th_ref, x2_ref, gfin_ref,
                  y_ref, acc_ref, gate_ref, coef_even_ref, coef_odd_ref, *, final_norm):
    j = pl.program_id(1)
    n_tiles = pl.num_programs(1) - 1
    n_groups, d, tg = xft_ref.shape
    te = u_ref.shape[0]
    keys_per_step = te // N_KEYS
    key0 = jnp.minimum(j, n_tiles - 1) * keys_per_step
    slab_rows_per_key = d // keys_per_step

    @pl.when(j == 0)
    def _():
        acc_ref[...] = jnp.zeros(acc_ref.shape, F32)
        coef_odd_ref[...] = jnp.zeros(coef_odd_ref.shape, BF16)

    def run(prev_ref, next_ref):
        def token_group(c, carry):
            def slab(r):
                acc_ref[c, r:r + ACC_SLAB, :] += jnp.dot(vt_ref[r:r + ACC_SLAB, :], prev_ref[c],
                                                         preferred_element_type=F32)

            for k in range(keys_per_step):
                rows = slice(k * N_KEYS, (k + 1) * N_KEYS)
                e1_rows = [e1_ref[c, h, pl.ds(key0 + k, 1), :] for h in range(PEER_HEADS)]
                low = None
                for lg in range(tg // LANES):
                    lanes = slice(lg * LANES, (lg + 1) * LANES)
                    gate = None
                    for h in range(PEER_HEADS):
                        prod = _rows_bf16(e1_rows[h][:, lanes]) * e2_ref[c, h, :, lanes]
                        sel = jnp.where(prod >= _rows_bf16(th_ref[c, h:h + 1, lanes]), prod, 0.0)
                        gate = sel if gate is None else gate + sel
                    gate_ref[rows, lanes] = gate
                    low = gate if low is None else jnp.minimum(low, gate)
                low = _fold_rows_min(low, 2 * SUBLANES)
                prev_ref[c, 0:2 * SUBLANES, 0:LANES] += jnp.minimum(low, 0.0)
                for r in range(k * slab_rows_per_key, (k + 1) * slab_rows_per_key, ACC_SLAB):
                    slab(r)
            act = _gelu(jnp.dot(u_ref[...], xft_ref[c], preferred_element_type=F32))
            next_ref[c] = gate_ref[...] * act.astype(BF16)
            return carry

        lax.fori_loop(0, n_groups, token_group, 0)

    @pl.when(j % 2 == 0)
    def _():
        run(coef_odd_ref, coef_even_ref)

    @pl.when(j % 2 == 1)
    def _():
        run(coef_even_ref, coef_odd_ref)

    @pl.when(j == n_tiles)
    def _():
        for c in range(n_groups):
            rows = slice(c * tg, (c + 1) * tg)
            x3 = x2_ref[rows, :] + acc_ref[c].T
            y_ref[rows, :] = _rms_norm(x3, gfin_ref[...]) if final_norm else x3


def _experts(xft, u_bf, vt_bf, e1, e2, th, x2, gfin, *, groups_per_tile, te, final_norm):
    t, d = x2.shape
    tg = xft.shape[2]
    gpt = groups_per_tile
    tm = gpt * tg
    n_exp = u_bf.shape[0]
    n_tiles = n_exp // te
    return pl.pallas_call(
        functools.partial(_experts_body, final_norm=final_norm),
        grid=(t // tm, n_tiles + 1),
        in_specs=[
            pl.BlockSpec((gpt, d, tg), lambda i, j: (i, 0, 0)),
            pl.BlockSpec((te, d), lambda i, j: (jnp.minimum(j, n_tiles - 1), 0)),
            pl.BlockSpec((d, te), lambda i, j: (0, jnp.maximum(j - 1, 0))),
            pl.BlockSpec((gpt, PEER_HEADS, N_KEYS, tg), lambda i, j: (i, 0, 0, 0)),
            pl.BlockSpec((gpt, PEER_HEADS, N_KEYS, tg), lambda i, j: (i, 0, 0, 0)),
            pl.BlockSpec((gpt, PEER_HEADS, tg), lambda i, j: (i, 0, 0)),
            pl.BlockSpec((tm, d), lambda i, j: (i, 0)),
            pl.BlockSpec((1, d), lambda i, j: (0, 0)),
        ],
        out_specs=pl.BlockSpec((tm, d), lambda i, j: (i, 0)),
        out_shape=jax.ShapeDtypeStruct((t, d), F32),
        scratch_shapes=[pltpu.VMEM((gpt, d, tg), F32), pltpu.VMEM((te, tg), BF16),
                        pltpu.VMEM((gpt, te, tg), BF16), pltpu.VMEM((gpt, te, tg), BF16)],
        compiler_params=pltpu.CompilerParams(
            dimension_semantics=("arbitrary", "arbitrary"), vmem_limit_bytes=VMEM_LIMIT),
        name="peer_experts",
    )(xft, u_bf, vt_bf, e1, e2, th, x2, gfin.reshape(1, d))


def _pick_tile(n, pref):
    t = pref
    while n % t:
        t //= 2
    return t


def kernel(x_prompt, x_sample, state_conv, state_lru, g_mix, w_in, gmlp_ws, gmlp_bs, conv_w, conv_b,
           lru_wa, lru_ba, lru_wx, lru_bx, lru_lambda, w_out, g_ffn, peer_wq, peer_keys, peer_u,
           peer_v, g_final):
    batch, seq, d = x_prompt.shape
    dec_batch, dec_seq, _ = x_sample.shape
    depth = w_in.shape[0]
    assert batch == 1 and seq % GMLP_CHUNK == 0 and dec_seq <= CAUSAL_CHUNK
    assert d // LANES == gmlp_ws.shape[1] == lru_wa.shape[1]
    tp, ts = batch * seq, dec_batch * dec_seq

    idx = jnp.arange(GMLP_CHUNK)
    mask = (idx[:, None] // CAUSAL_CHUNK) >= (idx[None, :] // CAUSAL_CHUNK)
    hk = PEER_HEADS * N_KEYS

    xp = x_prompt.reshape(tp, d)
    xs = x_sample.reshape(ts, d)
    conv_p, lru_p, conv_s, lru_s, vrows_s = [], [], [], [], []
    for l in range(depth):
        last = l == depth - 1
        w_in_bf = w_in[l].astype(BF16)
        ws_p = (gmlp_ws[l] * mask.astype(F32)).astype(BF16)
        ws_s = ws_p[:, :dec_seq, :dec_seq]
        bias = jnp.repeat(gmlp_bs[l].T, LANES, axis=1)
        shared = (conv_w[l], conv_b[l].reshape(1, d), lru_wa[l].astype(BF16), lru_ba[l].reshape(1, d),
                  lru_wx[l].astype(BF16), lru_bx[l].reshape(1, d), lru_lambda[l].reshape(1, d),
                  w_out[l].astype(BF16))
        wqt = peer_wq[l].reshape(d, PEER_HEADS, 2, N_KEYS).transpose(2, 1, 3, 0).reshape(2 * hk, d)
        wqt = wqt.astype(BF16)
        keys_bf = peer_keys[l].astype(BF16)
        u_bf = peer_u[l].astype(BF16)
        vt_bf = peer_v[l].T.astype(BF16)
        gfin = g_final if last else jnp.ones((d,), F32)

        def ffn(x2):
            t = x2.shape[0]
            tm, tg = _pick_tile(t, 512), _pick_tile(t, 256)
            xft, e1, e2, th = _route(x2, g_ffn[l], wqt, keys_bf, tm=tg)
            return _experts(xft, u_bf, vt_bf, e1, e2, th, x2, gfin, groups_per_tile=tm // tg,
                            te=512, final_norm=last)

        zp = _inproj(_prenorm(xp, g_mix[l], tm=_pick_tile(tp, 1024)), w_in_bf,
                     tm=_pick_tile(tp, 1024))
        zs = _inproj(_prenorm(xs, g_mix[l], tm=_pick_tile(ts, 1024)), w_in_bf,
                     tm=_pick_tile(ts, 1024))
        x2p, cp, lp = _mixer(zp, xp, 0, tp // GMLP_CHUNK, GMLP_CHUNK, ws_p, bias, *shared,
                             jnp.zeros((1, CONV_W - 1, d), F32), jnp.zeros((1, 1, d), F32),
                             carried=True)
        x2s, cs, ls = _mixer(zs, xs, 0, dec_batch, dec_seq, ws_s, bias[:dec_seq], *shared,
                             state_conv[l], state_lru[l].reshape(dec_batch, 1, d), carried=False)
        xp, xs = ffn(x2p), ffn(x2s)

        conv_p.append(cp)
        lru_p.append(lp.reshape(batch, d))
        conv_s.append(cs)
        lru_s.append(ls.reshape(dec_batch, d))
        vrows_s.append(zs[:, d:2 * d].reshape(dec_batch, dec_seq, d))

    return (xp.reshape(batch, seq, d), xs.reshape(dec_batch, dec_seq, d), jnp.stack(conv_p),
            jnp.stack(lru_p), jnp.stack(conv_s), jnp.stack(lru_s), jnp.stack(vrows_s))
```

```python
import functools
import math

import jax
import jax.numpy as jnp
from jax import lax
from jax.experimental import pallas as pl
from jax.experimental.pallas import tpu as pltpu

F32 = jnp.float32
BF16 = jnp.bfloat16

NORM_EPS = 1e-6
N_PARTS = 6
LANES = 128
SUBLANES = 8
CONV_W = 4
LRU_C = 8.0
GMLP_CHUNK = 128
CAUSAL_CHUNK = 64
N_KEYS = 128
PEER_HEADS = 8
PEER_TOPK = 16
VMEM_LIMIT = 56 * 1024 * 1024
ACC_SLAB = 512
INPROJ_SLAB = 128


def _gelu(x):
    c = math.sqrt(2.0 / math.pi)
    return x * (0.5 * (1.0 + jnp.tanh(c * (x + 0.044715 * (x * x * x)))))


def _rms_norm(x, g):
    ms = jnp.mean(x * x, axis=-1, keepdims=True)
    return (x * lax.rsqrt(ms + NORM_EPS)) * g


def _prenorm_body(x_ref, g_ref, xn_ref):
    xn_ref[...] = _rms_norm(x_ref[...], g_ref[...]).astype(BF16)


def _prenorm(x, g, *, tm):
    t, d = x.shape
    return pl.pallas_call(
        _prenorm_body,
        grid=(t // tm,),
        in_specs=[pl.BlockSpec((tm, d), lambda i: (i, 0)), pl.BlockSpec((1, d), lambda i: (0, 0))],
        out_specs=pl.BlockSpec((tm, d), lambda i: (i, 0)),
        out_shape=jax.ShapeDtypeStruct((t, d), BF16),
        compiler_params=pltpu.CompilerParams(
            dimension_semantics=("arbitrary",), vmem_limit_bytes=VMEM_LIMIT),
        name="prenorm",
    )(x, g.reshape(1, d))


def _inproj_body(xn_ref, w_ref, z_ref):
    part = pl.program_id(0)

    def project(activation):
        for r in range(0, z_ref.shape[0], INPROJ_SLAB):
            rows = slice(r, r + INPROJ_SLAB)
            z_ref[rows, :] = activation(
                jnp.dot(xn_ref[rows, :], w_ref[...], preferred_element_type=F32))

    is_gelu = jnp.logical_or(part <= 1, part == 3)

    @pl.when(is_gelu)
    def _():
        project(_gelu)

    @pl.when(part == 2)
    def _():
        project(lambda z: z)

    @pl.when(part >= 4)
    def _():
        project(jax.nn.sigmoid)


def _inproj(xn, w_bf, *, tm):
    t, d = xn.shape
    n = w_bf.shape[1]
    tn = n // N_PARTS
    return pl.pallas_call(
        _inproj_body,
        grid=(N_PARTS, t // tm),
        in_specs=[
            pl.BlockSpec((tm, d), lambda p, i: (i, 0)),
            pl.BlockSpec((d, tn), lambda p, i: (0, p)),
        ],
        out_specs=pl.BlockSpec((tm, tn), lambda p, i: (i, p)),
        out_shape=jax.ShapeDtypeStruct((t, n), F32),
        compiler_params=pltpu.CompilerParams(
            dimension_semantics=("arbitrary", "arbitrary"), vmem_limit_bytes=VMEM_LIMIT),
        name="inproj",
    )(xn, w_bf)


def _scan_rows(a, b, h_in):
    sub = lax.broadcasted_iota(jnp.int32, (SUBLANES, a.shape[1]), 0)
    keep = {s: sub >= s for s in (1, 2, 4)}
    blocks = []
    carry = h_in
    for r in range(0, a.shape[0], SUBLANES):
        ab, bb = a[r:r + SUBLANES, :], b[r:r + SUBLANES, :]
        for s in (1, 2, 4):
            a_sh = jnp.where(keep[s], pltpu.roll(ab, s, axis=0), 1.0)
            b_sh = jnp.where(keep[s], pltpu.roll(bb, s, axis=0), 0.0)
            bb = ab * b_sh + bb
            ab = ab * a_sh
        hb = ab * carry + bb
        carry = hb[SUBLANES - 1:SUBLANES, :]
        blocks.append(hb)
    return jnp.concatenate(blocks, axis=0), carry


def _mixer_body(zu_ref, zv_ref, zx_ref, zg_ref, zga_ref, zgb_ref, x_ref,
                wg_ref, bg_ref, cw_ref, cb_ref, wa_ref, ba_ref, wx_ref, bx_ref, lam_ref,
                wout_ref, prev_ref, h0_ref,
                x2_ref, conv_ref, hlast_ref,
                xp_ref, merged_ref, hc_ref, *, carried):
    c = pl.program_id(0)
    rows_c = zx_ref.shape[0]
    pad = SUBLANES

    if carried:
        @pl.when(c == 0)
        def _():
            xp_ref[0:pad, :] = jnp.zeros((pad, xp_ref.shape[1]), F32)
            hc_ref[...] = jnp.zeros(hc_ref.shape, F32)
    else:
        xp_ref[pad - (CONV_W - 1):pad, :] = prev_ref[0]
        hc_ref[0:1, :] = h0_ref[0]
    xp_ref[pad:pad + rows_c, :] = zx_ref[...]

    def group(j, carry):
        col = pl.ds(pl.multiple_of(j * LANES, LANES), LANES)
        s = jnp.dot(wg_ref[j], zv_ref[:, col].astype(BF16), preferred_element_type=F32) + bg_ref[:, col]
        y_a = zu_ref[:, col] * s
        xc = cb_ref[:, col]
        for k in range(CONV_W):
            lo = pad - (CONV_W - 1) + k
            xc = xc + xp_ref[lo:lo + rows_c, col] * cw_ref[k:k + 1, col]
        xcb = xc.astype(BF16)
        r = jax.nn.sigmoid(jnp.dot(xcb, wa_ref[j], preferred_element_type=F32) + ba_ref[:, col])
        i = jax.nn.sigmoid(jnp.dot(xcb, wx_ref[j], preferred_element_type=F32) + bx_ref[:, col])
        nlam = -lam_ref[:, col]
        softplus = jnp.maximum(nlam, 0.0) + jnp.log1p(jnp.exp(-jnp.abs(nlam)))
        log_a = (-LRU_C) * r * softplus
        a = jnp.exp(log_a)
        mult = jnp.sqrt(jnp.tanh(-log_a) * (a * a + 1.0))
        if carried:
            rows = lax.broadcasted_iota(jnp.int32, mult.shape, 0)
            mult = jnp.where(jnp.logical_and(rows == 0, c == 0), 1.0, mult)
        b = mult * i * xc
        h, h_last = _scan_rows(a, b, hc_ref[0:1, col])
        hc_ref[0:1, col] = h_last
        y_b = h * zg_ref[:, col]
        merged_ref[:, col] = (zga_ref[:, col] * y_a + zgb_ref[:, col] * y_b).astype(BF16)
        return carry

    lax.fori_loop(0, zx_ref.shape[1] // LANES, group, 0, unroll=2)

    x2_ref[...] = x_ref[...] + jnp.dot(merged_ref[...], wout_ref[...], preferred_element_type=F32)
    conv_ref[0] = xp_ref[pad + rows_c - (CONV_W - 1):pad + rows_c, :]
    hlast_ref[0] = hc_ref[0:1, :]
    if carried:
        xp_ref[0:pad, :] = xp_ref[rows_c:rows_c + pad, :]


def _mixer(z, x, row0, n_seq, rows_c, wg, bg, cw, cb, wa, ba, wx, bx, lam, wout, prev, h0, *, carried):
    d = x.shape[1]
    blk0 = row0 // rows_c
    n_state = prev.shape[0]

    def zspec(part):
        return pl.BlockSpec((rows_c, d), lambda c, part=part: (blk0 + c, part))

    def const(shape):
        return pl.BlockSpec(shape, lambda c: (0,) * len(shape))

    def state_idx(c):
        return (c, 0, 0) if not carried else (0, 0, 0)

    return pl.pallas_call(
        functools.partial(_mixer_body, carried=carried),
        grid=(n_seq,),
        in_specs=[zspec(p) for p in range(N_PARTS)] + [
            pl.BlockSpec((rows_c, d), lambda c: (blk0 + c, 0)),
            const(wg.shape), const(bg.shape), const(cw.shape), const(cb.shape),
            const(wa.shape), const(ba.shape), const(wx.shape), const(bx.shape), const(lam.shape),
            const(wout.shape),
            pl.BlockSpec((1, CONV_W - 1, d), state_idx),
            pl.BlockSpec((1, 1, d), state_idx),
        ],
        out_specs=[
            pl.BlockSpec((rows_c, d), lambda c: (c, 0)),
            pl.BlockSpec((1, CONV_W - 1, d), state_idx),
            pl.BlockSpec((1, 1, d), state_idx),
        ],
        out_shape=[
            jax.ShapeDtypeStruct((n_seq * rows_c, d), F32),
            jax.ShapeDtypeStruct((n_state, CONV_W - 1, d), F32),
            jax.ShapeDtypeStruct((n_state, 1, d), F32),
        ],
        scratch_shapes=[
            pltpu.VMEM((rows_c + 2 * SUBLANES, d), F32),
            pltpu.VMEM((rows_c, d), BF16),
            pltpu.VMEM((SUBLANES, d), F32),
        ],
        compiler_params=pltpu.CompilerParams(
            dimension_semantics=("arbitrary",), vmem_limit_bytes=VMEM_LIMIT),
        name="mixer_stream" if carried else "mixer_step",
    )(z, z, z, z, z, z, x, wg, bg, cw, cb, wa, ba, wx, bx, lam, wout, prev, h0)


def _sort_network(n):
    pairs = []
    p = 1
    while p < n:
        k = p
        while k >= 1:
            for j in range(k % p, n - k, 2 * k):
                for i in range(min(k, n - j - k)):
                    if (i + j) // (2 * p) == (i + j + k) // (2 * p):
                        pairs.append((i + j, i + j + k))
            k //= 2
        p *= 2
    return pairs


_SORT16 = _sort_network(PEER_TOPK)


def _sort_desc(v):
    v = list(v)
    for i, j in _SORT16:
        v[i], v[j] = jnp.maximum(v[i], v[j]), jnp.minimum(v[i], v[j])
    return v


def _merge_top(xs, ys):
    z = [jnp.maximum(xs[i], ys[PEER_TOPK - 1 - i]) for i in range(PEER_TOPK)]
    d = PEER_TOPK // 2
    while d >= 1:
        for i in range(PEER_TOPK):
            if (i & d) == 0:
                z[i], z[i + d] = jnp.maximum(z[i], z[i + d]), jnp.minimum(z[i], z[i + d])
        d //= 2
    return z


def _top_of_lists(lists):
    while len(lists) > 1:
        lists = [_merge_top(lists[i], lists[i + 1]) for i in range(0, len(lists), 2)]
    return lists[0]


def _round_bf16(x):
    return x.astype(BF16).astype(F32)


def _route_body(x2_ref, g_ref, wqt_ref, keys_ref, xft_ref, e1_ref, e2_ref, th_ref, s_ref):
    tm = x2_ref.shape[0]
    hk = PEER_HEADS * N_KEYS
    xft = _rms_norm(x2_ref[...], g_ref[...]).T.astype(BF16)
    xft_ref[0] = xft
    qt = jnp.dot(wqt_ref[...], xft, preferred_element_type=F32)
    for p in range(2):
        for h in range(PEER_HEADS):
            lo = p * hk + h * N_KEYS
            q_hp = qt[lo:lo + N_KEYS, :].astype(BF16)
            s_hp = jnp.dot(keys_ref[p], q_hp, preferred_element_type=F32)
            for g in range(tm // LANES):
                s_ref[p, g, h * N_KEYS:(h + 1) * N_KEYS, :] = s_hp[:, g * LANES:(g + 1) * LANES]

    def lane_group(lg, carry):
        lanes = pl.ds(pl.multiple_of(lg * LANES, LANES), LANES)
        tops = []
        for p in range(2):
            vals = [s_ref[p, lg, pl.ds(n, PEER_HEADS, stride=N_KEYS), :] for n in range(N_KEYS)]
            groups = [_sort_desc(vals[i:i + PEER_TOPK]) for i in range(0, N_KEYS, PEER_TOPK)]
            tops.append(_top_of_lists(groups))
        m1, m2 = tops[0][0], tops[1][0]
        e1 = [jnp.exp(v - m1) for v in tops[0]]
        e2 = [jnp.exp(v - m2) for v in tops[1]]
        best = _top_of_lists([[e1[i] * e2[j] for j in range(PEER_TOPK)] for i in range(PEER_TOPK)])
        z = best[0]
        for v in best[1:]:
            z = z + v
        rz = 1.0 / z
        e1n = [_round_bf16(v * rz) for v in e1]
        e2n = [_round_bf16(v) for v in e2]
        bestn = _top_of_lists([[_round_bf16(e1n[i] * e2n[j]) for j in range(PEER_TOPK)]
                               for i in range(PEER_TOPK)])
        th_ref[0, :, lanes] = bestn[PEER_TOPK - 1]
        for h in range(PEER_HEADS):
            rows = slice(h * N_KEYS, (h + 1) * N_KEYS)
            e1_ref[0, h, :, lanes] = jnp.exp(s_ref[0, lg, rows, :] - m1[h:h + 1, :]) * rz[h:h + 1, :]
            e2_ref[0, h, :, lanes] = jnp.exp(s_ref[1, lg, rows, :] - m2[h:h + 1, :]).astype(BF16)
        return carry

    lax.fori_loop(0, tm // LANES, lane_group, 0)


def _route(x2, g, wqt, keys_bf, *, tm):
    t, d = x2.shape
    hk = PEER_HEADS * N_KEYS
    n_groups = t // tm
    return pl.pallas_call(
        _route_body,
        grid=(t // tm,),
        in_specs=[
            pl.BlockSpec((tm, d), lambda i: (i, 0)),
            pl.BlockSpec((1, d), lambda i: (0, 0)),
            pl.BlockSpec(wqt.shape, lambda i: (0, 0)),
            pl.BlockSpec(keys_bf.shape, lambda i: (0, 0, 0)),
        ],
        out_specs=[
            pl.BlockSpec((1, d, tm), lambda i: (i, 0, 0)),
            pl.BlockSpec((1, PEER_HEADS, N_KEYS, tm), lambda i: (i, 0, 0, 0)),
            pl.BlockSpec((1, PEER_HEADS, N_KEYS, tm), lambda i: (i, 0, 0, 0)),
            pl.BlockSpec((1, PEER_HEADS, tm), lambda i: (i, 0, 0)),
        ],
        out_shape=[
            jax.ShapeDtypeStruct((n_groups, d, tm), BF16),
            jax.ShapeDtypeStruct((n_groups, PEER_HEADS, N_KEYS, tm), F32),
            jax.ShapeDtypeStruct((n_groups, PEER_HEADS, N_KEYS, tm), BF16),
            jax.ShapeDtypeStruct((n_groups, PEER_HEADS, tm), F32),
        ],
        scratch_shapes=[pltpu.VMEM((2, tm // LANES, hk, LANES), F32)],
        compiler_params=pltpu.CompilerParams(
            dimension_semantics=("arbitrary",), vmem_limit_bytes=VMEM_LIMIT),
        name="peer_route",
    )(x2, g.reshape(1, d), wqt, keys_bf)


def _rows_bf16(row):
    tile = jnp.broadcast_to(row, (2 * SUBLANES, row.shape[1])).astype(BF16)
    return pltpu.repeat(tile, N_KEYS // (2 * SUBLANES), axis=0)


def _fold_rows_min(x, rows):
    while x.shape[0] > rows:
        half = x.shape[0] // 2
        x = jnp.minimum(x[:half], x[half:])
    return x


def _experts_body(xft_ref, u_ref, vt_ref, e1_ref, e2_ref, th_ref, x2_ref, gfin_ref,
                  y_ref, acc_ref, gate_ref, coef_even_ref, coef_odd_ref, *, n_tiles, final_norm):
    s = pl.program_id(0)
    last = pl.num_programs(0) - 1
    n_groups, d, tg = xft_ref.shape
    te = u_ref.shape[0]
    keys_per_step = te // N_KEYS
    j = s % n_tiles
    key0 = j * keys_per_step
    slab_rows_per_key = d // keys_per_step
    assert n_tiles % 2 == 0

    def project_values(c, prev_ref, rows):
        step = min(ACC_SLAB, rows.stop - rows.start)
        for r in range(rows.start, rows.stop, step):
            acc_ref[c, r:r + step, :] += jnp.dot(vt_ref[r:r + step, :], prev_ref[c],
                                                 preferred_element_type=F32)

    def route_weights(c, k):
        rows = slice(k * N_KEYS, (k + 1) * N_KEYS)
        e1_rows = [e1_ref[c, h, pl.ds(key0 + k, 1), :] for h in range(PEER_HEADS)]
        low = None
        for lg in range(tg // LANES):
            lanes = slice(lg * LANES, (lg + 1) * LANES)
            gate = None
            for h in range(PEER_HEADS):
                prod = _rows_bf16(e1_rows[h][:, lanes]) * e2_ref[c, h, :, lanes]
                sel = jnp.where(prod >= _rows_bf16(th_ref[c, h:h + 1, lanes]), prod, 0.0)
                gate = sel if gate is None else gate + sel
            gate_ref[c, rows, lanes] = gate
            low = gate if low is None else jnp.minimum(low, gate)
        return _fold_rows_min(low, 2 * SUBLANES)

    def weighted_activations(c, next_ref):
        act = _gelu(jnp.dot(u_ref[...], xft_ref[c], preferred_element_type=F32))
        next_ref[c] = gate_ref[c] * act.astype(BF16)

    def steady(prev_ref, next_ref):
        for c in range(n_groups):
            for k in range(keys_per_step):
                low = route_weights(c, k)
                prev_ref[c, 0:2 * SUBLANES, 0:LANES] += jnp.minimum(low, 0.0)
                project_values(c, prev_ref, slice(k * slab_rows_per_key, (k + 1) * slab_rows_per_key))
            weighted_activations(c, next_ref)

    @pl.when(jnp.logical_and(j != 0, j % 2 == 0))
    def _():
        steady(coef_odd_ref, coef_even_ref)

    @pl.when(j % 2 == 1)
    def _():
        steady(coef_even_ref, coef_odd_ref)

    @pl.when(jnp.logical_and(j == 0, s > 0))
    def _():
        for c in range(n_groups):
            project_values(c, coef_odd_ref, slice(0, d))
        for c in range(n_groups):
            rows = slice(c * tg, (c + 1) * tg)
            x3 = x2_ref[rows, :] + acc_ref[c].T
            y_ref[rows, :] = _rms_norm(x3, gfin_ref[...]) if final_norm else x3

    @pl.when(jnp.logical_and(j == 0, s < last))
    def _():
        acc_ref[...] = jnp.zeros(acc_ref.shape, F32)
        for c in range(n_groups):
            for k in range(keys_per_step):
                route_weights(c, k)
            weighted_activations(c, coef_even_ref)


def _experts(xft, u_bf, vt_bf, e1, e2, th, x2, gfin, *, groups_per_tile, te, final_norm):
    t, d = x2.shape
    tg = xft.shape[2]
    gpt = groups_per_tile
    tm = gpt * tg
    n_exp = u_bf.shape[0]
    n_tiles = n_exp // te
    n_tok = t // tm
    n_steps = n_tok * n_tiles + 1

    def tok_now(s):
        return jnp.minimum(s // n_tiles, n_tok - 1)

    def tok_prev(s):
        return jnp.maximum(s - 1, 0) // n_tiles

    return pl.pallas_call(
        functools.partial(_experts_body, n_tiles=n_tiles, final_norm=final_norm),
        grid=(n_steps,),
        in_specs=[
            pl.BlockSpec((gpt, d, tg), lambda s: (tok_now(s), 0, 0)),
            pl.BlockSpec((te, d), lambda s: (s % n_tiles, 0)),
            pl.BlockSpec((d, te), lambda s: (0, jnp.maximum(s - 1, 0) % n_tiles)),
            pl.BlockSpec((gpt, PEER_HEADS, N_KEYS, tg), lambda s: (tok_now(s), 0, 0, 0)),
            pl.BlockSpec((gpt, PEER_HEADS, N_KEYS, tg), lambda s: (tok_now(s), 0, 0, 0)),
            pl.BlockSpec((gpt, PEER_HEADS, tg), lambda s: (tok_now(s), 0, 0)),
            pl.BlockSpec((tm, d), lambda s: (tok_prev(s), 0)),
            pl.BlockSpec((1, d), lambda s: (0, 0)),
        ],
        out_specs=pl.BlockSpec((tm, d), lambda s: (tok_prev(s), 0)),
        out_shape=jax.ShapeDtypeStruct((t, d), F32),
        scratch_shapes=[pltpu.VMEM((gpt, d, tg), F32), pltpu.VMEM((gpt, te, tg), BF16),
                        pltpu.VMEM((gpt, te, tg), BF16), pltpu.VMEM((gpt, te, tg), BF16)],
        compiler_params=pltpu.CompilerParams(
            dimension_semantics=("arbitrary",), vmem_limit_bytes=VMEM_LIMIT),
        name="peer_experts",
    )(xft, u_bf, vt_bf, e1, e2, th, x2, gfin.reshape(1, d))


def _pick_tile(n, pref):
    t = pref
    while n % t:
        t //= 2
    return t


def kernel(x_prompt, x_sample, state_conv, state_lru, g_mix, w_in, gmlp_ws, gmlp_bs, conv_w, conv_b,
           lru_wa, lru_ba, lru_wx, lru_bx, lru_lambda, w_out, g_ffn, peer_wq, peer_keys, peer_u,
           peer_v, g_final):
    batch, seq, d = x_prompt.shape
    dec_batch, dec_seq, _ = x_sample.shape
    depth = w_in.shape[0]
    assert batch == 1 and seq % GMLP_CHUNK == 0 and dec_seq <= CAUSAL_CHUNK
    assert d // LANES == gmlp_ws.shape[1] == lru_wa.shape[1]
    tp, ts = batch * seq, dec_batch * dec_seq

    idx = jnp.arange(GMLP_CHUNK)
    mask = (idx[:, None] // CAUSAL_CHUNK) >= (idx[None, :] // CAUSAL_CHUNK)
    hk = PEER_HEADS * N_KEYS

    xp = x_prompt.reshape(tp, d)
    xs = x_sample.reshape(ts, d)
    conv_p, lru_p, conv_s, lru_s, vrows_s = [], [], [], [], []
    for l in range(depth):
        last = l == depth - 1
        w_in_bf = w_in[l].astype(BF16)
        ws_p = (gmlp_ws[l] * mask.astype(F32)).astype(BF16)
        ws_s = ws_p[:, :dec_seq, :dec_seq]
        bias = jnp.repeat(gmlp_bs[l].T, LANES, axis=1)
        shared = (conv_w[l], conv_b[l].reshape(1, d), lru_wa[l].astype(BF16), lru_ba[l].reshape(1, d),
                  lru_wx[l].astype(BF16), lru_bx[l].reshape(1, d), lru_lambda[l].reshape(1, d),
                  w_out[l].astype(BF16))
        wqt = peer_wq[l].reshape(d, PEER_HEADS, 2, N_KEYS).transpose(2, 1, 3, 0).reshape(2 * hk, d)
        wqt = wqt.astype(BF16)
        keys_bf = peer_keys[l].astype(BF16)
        u_bf = peer_u[l].astype(BF16)
        vt_bf = peer_v[l].T.astype(BF16)
        gfin = g_final if last else jnp.ones((d,), F32)

        def ffn(x2):
            t = x2.shape[0]
            tm, tg = _pick_tile(t, 512), _pick_tile(t, 256)
            xft, e1, e2, th = _route(x2, g_ffn[l], wqt, keys_bf, tm=tg)
            return _experts(xft, u_bf, vt_bf, e1, e2, th, x2, gfin, groups_per_tile=tm // tg,
                            te=1024, final_norm=last)

        zp = _inproj(_prenorm(xp, g_mix[l], tm=_pick_tile(tp, 1024)), w_in_bf,
                     tm=_pick_tile(tp, 1024))
        zs = _inproj(_prenorm(xs, g_mix[l], tm=_pick_tile(ts, 1024)), w_in_bf,
                     tm=_pick_tile(ts, 1024))
        x2p, cp, lp = _mixer(zp, xp, 0, tp // GMLP_CHUNK, GMLP_CHUNK, ws_p, bias, *shared,
                             jnp.zeros((1, CONV_W - 1, d), F32), jnp.zeros((1, 1, d), F32),
                             carried=True)
        x2s, cs, ls = _mixer(zs, xs, 0, dec_batch, dec_seq, ws_s, bias[:dec_seq], *shared,
                             state_conv[l], state_lru[l].reshape(dec_batch, 1, d), carried=False)
        xp, xs = ffn(x2p), ffn(x2s)

        conv_p.append(cp)
        lru_p.append(lp.reshape(batch, d))
        conv_s.append(cs)
        lru_s.append(ls.reshape(dec_batch, d))
        vrows_s.append(zs[:, d:2 * d].reshape(dec_batch, dec_seq, d))

    return (xp.reshape(batch, seq, d), xs.reshape(dec_batch, dec_seq, d), jnp.stack(conv_p),
            jnp.stack(lru_p), jnp.stack(conv_s), jnp.stack(lru_s), jnp.stack(vrows_s))
```

```python
import functools
import math

import jax
import jax.numpy as jnp
from jax import lax
from jax.experimental import pallas as pl
from jax.experimental.pallas import tpu as pltpu

F32 = jnp.float32
BF16 = jnp.bfloat16

NORM_EPS = 1e-6
N_PARTS = 6
LANES = 128
SUBLANES = 8
CONV_W = 4
LRU_C = 8.0
GMLP_CHUNK = 128
CAUSAL_CHUNK = 64
N_KEYS = 128
PEER_HEADS = 8
PEER_TOPK = 16
VMEM_LIMIT = 56 * 1024 * 1024
ACC_SLAB = 512
INPROJ_SLAB = 256


def _gelu(x):
    c = math.sqrt(2.0 / math.pi)
    return x * (0.5 * (1.0 + jnp.tanh(c * (x + 0.044715 * (x * x * x)))))


def _rms_norm(x, g):
    ms = jnp.mean(x * x, axis=-1, keepdims=True)
    return (x * lax.rsqrt(ms + NORM_EPS)) * g


def _prenorm_body(x_ref, g_ref, xn_ref):
    xn_ref[...] = _rms_norm(x_ref[...], g_ref[...]).astype(BF16)


def _prenorm(x, g, *, tm):
    t, d = x.shape
    return pl.pallas_call(
        _prenorm_body,
        grid=(t // tm,),
        in_specs=[pl.BlockSpec((tm, d), lambda i: (i, 0)), pl.BlockSpec((1, d), lambda i: (0, 0))],
        out_specs=pl.BlockSpec((tm, d), lambda i: (i, 0)),
        out_shape=jax.ShapeDtypeStruct((t, d), BF16),
        compiler_params=pltpu.CompilerParams(
            dimension_semantics=("arbitrary",), vmem_limit_bytes=VMEM_LIMIT),
        name="prenorm",
    )(x, g.reshape(1, d))


def _inproj_body(xn_ref, w_ref, z_ref):
    part = pl.program_id(0)

    def project(activation):
        for r in range(0, z_ref.shape[0], INPROJ_SLAB):
            rows = slice(r, r + INPROJ_SLAB)
            z_ref[rows, :] = activation(
                jnp.dot(xn_ref[rows, :], w_ref[...], preferred_element_type=F32))

    is_gelu = jnp.logical_or(part <= 1, part == 3)

    @pl.when(is_gelu)
    def _():
        project(_gelu)

    @pl.when(part == 2)
    def _():
        project(lambda z: z)

    @pl.when(part >= 4)
    def _():
        project(jax.nn.sigmoid)


def _inproj(xn, w_bf, *, tm):
    t, d = xn.shape
    n = w_bf.shape[1]
    tn = n // N_PARTS
    return pl.pallas_call(
        _inproj_body,
        grid=(N_PARTS, t // tm),
        in_specs=[
            pl.BlockSpec((tm, d), lambda p, i: (i, 0)),
            pl.BlockSpec((d, tn), lambda p, i: (0, p)),
        ],
        out_specs=pl.BlockSpec((tm, tn), lambda p, i: (i, p)),
        out_shape=jax.ShapeDtypeStruct((t, n), F32),
        compiler_params=pltpu.CompilerParams(
            dimension_semantics=("arbitrary", "arbitrary"), vmem_limit_bytes=VMEM_LIMIT),
        name="inproj",
    )(xn, w_bf)


def _scan_rows(a, b, h_in):
    sub = lax.broadcasted_iota(jnp.int32, (SUBLANES, a.shape[1]), 0)
    keep = {s: sub >= s for s in (1, 2, 4)}
    blocks = []
    carry = h_in
    for r in range(0, a.shape[0], SUBLANES):
        ab, bb = a[r:r + SUBLANES, :], b[r:r + SUBLANES, :]
        for s in (1, 2, 4):
            a_sh = jnp.where(keep[s], pltpu.roll(ab, s, axis=0), 1.0)
            b_sh = jnp.where(keep[s], pltpu.roll(bb, s, axis=0), 0.0)
            bb = ab * b_sh + bb
            ab = ab * a_sh
        hb = ab * carry + bb
        carry = hb[SUBLANES - 1:SUBLANES, :]
        blocks.append(hb)
    return jnp.concatenate(blocks, axis=0), carry


def _mixer_body(zu_ref, zv_ref, zx_ref, zg_ref, zga_ref, zgb_ref, x_ref,
                wg_ref, bg_ref, cw_ref, cb_ref, wa_ref, ba_ref, wx_ref, bx_ref, lam_ref,
                wout_ref, prev_ref, h0_ref,
                x2_ref, conv_ref, hlast_ref,
                xp_ref, merged_ref, hc_ref, *, carried):
    c = pl.program_id(0)
    rows_c = zx_ref.shape[0]
    pad = SUBLANES
    n_groups = zx_ref.shape[1] // LANES

    if carried:
        @pl.when(c == 0)
        def _():
            xp_ref[0:pad, :] = jnp.zeros((pad, xp_ref.shape[1]), F32)
            hc_ref[...] = jnp.zeros(hc_ref.shape, F32)
    else:
        xp_ref[pad - (CONV_W - 1):pad, :] = prev_ref[0]
        hc_ref[0:1, :] = h0_ref[0]
    xp_ref[pad:pad + rows_c, :] = zx_ref[...]

    def group(j, carry):
        col = pl.ds(pl.multiple_of(j * LANES, LANES), LANES)
        s = jnp.dot(wg_ref[j], zv_ref[:, col].astype(BF16), preferred_element_type=F32) + bg_ref[:, col]
        y_a = zu_ref[:, col] * s
        xc = cb_ref[:, col]
        for k in range(CONV_W):
            lo = pad - (CONV_W - 1) + k
            xc = xc + xp_ref[lo:lo + rows_c, col] * cw_ref[k:k + 1, col]
        xcb = xc.astype(BF16)
        r = jax.nn.sigmoid(jnp.dot(xcb, wa_ref[j], preferred_element_type=F32) + ba_ref[:, col])
        i = jax.nn.sigmoid(jnp.dot(xcb, wx_ref[j], preferred_element_type=F32) + bx_ref[:, col])
        nlam = -lam_ref[:, col]
        softplus = jnp.maximum(nlam, 0.0) + jnp.log1p(jnp.exp(-jnp.abs(nlam)))
        log_a = (-LRU_C) * r * softplus
        a = jnp.exp(log_a)
        mult = jnp.sqrt(jnp.tanh(-log_a) * (a * a + 1.0))
        if carried:
            rows = lax.broadcasted_iota(jnp.int32, mult.shape, 0)
            mult = jnp.where(jnp.logical_and(rows == 0, c == 0), 1.0, mult)
        b = mult * i * xc
        h, h_last = _scan_rows(a, b, hc_ref[0:1, col])
        hc_ref[0:1, col] = h_last
        y_b = h * zg_ref[:, col]
        merged_ref[:, col] = (zga_ref[:, col] * y_a + zgb_ref[:, col] * y_b).astype(BF16)
        return carry

    lax.fori_loop(0, n_groups, group, 0, unroll=4)

    x2_ref[...] = x_ref[...] + jnp.dot(merged_ref[...], wout_ref[...], preferred_element_type=F32)
    conv_ref[0] = xp_ref[pad + rows_c - (CONV_W - 1):pad + rows_c, :]
    hlast_ref[0] = hc_ref[0:1, :]
    if carried:
        xp_ref[0:pad, :] = xp_ref[rows_c:rows_c + pad, :]


def _mixer(z, x, row0, n_seq, rows_c, wg, bg, cw, cb, wa, ba, wx, bx, lam, wout, prev, h0, *, carried):
    d = x.shape[1]
    blk0 = row0 // rows_c
    n_state = prev.shape[0]

    def zspec(part):
        return pl.BlockSpec((rows_c, d), lambda c, part=part: (blk0 + c, part))

    def const(shape):
        return pl.BlockSpec(shape, lambda c: (0,) * len(shape))

    def state_idx(c):
        return (c, 0, 0) if not carried else (0, 0, 0)

    return pl.pallas_call(
        functools.partial(_mixer_body, carried=carried),
        grid=(n_seq,),
        in_specs=[zspec(p) for p in range(N_PARTS)] + [
            pl.BlockSpec((rows_c, d), lambda c: (blk0 + c, 0)),
            const(wg.shape), const(bg.shape), const(cw.shape), const(cb.shape),
            const(wa.shape), const(ba.shape), const(wx.shape), const(bx.shape), const(lam.shape),
            const(wout.shape),
            pl.BlockSpec((1, CONV_W - 1, d), state_idx),
            pl.BlockSpec((1, 1, d), state_idx),
        ],
        out_specs=[
            pl.BlockSpec((rows_c, d), lambda c: (c, 0)),
            pl.BlockSpec((1, CONV_W - 1, d), state_idx),
            pl.BlockSpec((1, 1, d), state_idx),
        ],
        out_shape=[
            jax.ShapeDtypeStruct((n_seq * rows_c, d), F32),
            jax.ShapeDtypeStruct((n_state, CONV_W - 1, d), F32),
            jax.ShapeDtypeStruct((n_state, 1, d), F32),
        ],
        scratch_shapes=[
            pltpu.VMEM((rows_c + 2 * SUBLANES, d), F32),
            pltpu.VMEM((rows_c, d), BF16),
            pltpu.VMEM((SUBLANES, d), F32),
        ],
        compiler_params=pltpu.CompilerParams(
            dimension_semantics=("arbitrary",), vmem_limit_bytes=VMEM_LIMIT),
        name="mixer_stream" if carried else "mixer_step",
    )(z, z, z, z, z, z, x, wg, bg, cw, cb, wa, ba, wx, bx, lam, wout, prev, h0)


def _sort_network(n):
    pairs = []
    p = 1
    while p < n:
        k = p
        while k >= 1:
            for j in range(k % p, n - k, 2 * k):
                for i in range(min(k, n - j - k)):
                    if (i + j) // (2 * p) == (i + j + k) // (2 * p):
                        pairs.append((i + j, i + j + k))
            k //= 2
        p *= 2
    return pairs


_SORT16 = _sort_network(PEER_TOPK)


def _sort_desc(v):
    v = list(v)
    for i, j in _SORT16:
        v[i], v[j] = jnp.maximum(v[i], v[j]), jnp.minimum(v[i], v[j])
    return v


def _merge_top(xs, ys):
    z = [jnp.maximum(xs[i], ys[PEER_TOPK - 1 - i]) for i in range(PEER_TOPK)]
    d = PEER_TOPK // 2
    while d >= 1:
        for i in range(PEER_TOPK):
            if (i & d) == 0:
                z[i], z[i + d] = jnp.maximum(z[i], z[i + d]), jnp.minimum(z[i], z[i + d])
        d //= 2
    return z


def _top_of_lists(lists):
    while len(lists) > 1:
        lists = [_merge_top(lists[i], lists[i + 1]) for i in range(0, len(lists), 2)]
    return lists[0]


def _round_bf16(x):
    return x.astype(BF16).astype(F32)


def _route_body(x2_ref, g_ref, wqt_ref, keys_ref, xft_ref, e1_ref, e2_ref, th_ref, s_ref):
    tm = x2_ref.shape[0]
    hk = PEER_HEADS * N_KEYS
    xft = _rms_norm(x2_ref[...], g_ref[...]).T.astype(BF16)
    xft_ref[0] = xft
    qt = jnp.dot(wqt_ref[...], xft, preferred_element_type=F32)
    for p in range(2):
        for h in range(PEER_HEADS):
            lo = p * hk + h * N_KEYS
            q_hp = qt[lo:lo + N_KEYS, :].astype(BF16)
            s_hp = jnp.dot(keys_ref[p], q_hp, preferred_element_type=F32)
            for g in range(tm // LANES):
                s_ref[p, g, h * N_KEYS:(h + 1) * N_KEYS, :] = s_hp[:, g * LANES:(g + 1) * LANES]

    def lane_group(lg, carry):
        lanes = pl.ds(pl.multiple_of(lg * LANES, LANES), LANES)
        tops = []
        for p in range(2):
            vals = [s_ref[p, lg, pl.ds(n, PEER_HEADS, stride=N_KEYS), :] for n in range(N_KEYS)]
            groups = [_sort_desc(vals[i:i + PEER_TOPK]) for i in range(0, N_KEYS, PEER_TOPK)]
            tops.append(_top_of_lists(groups))
        m1, m2 = tops[0][0], tops[1][0]
        e1 = [jnp.exp(v - m1) for v in tops[0]]
        e2 = [jnp.exp(v - m2) for v in tops[1]]
        best = _top_of_lists([[e1[i] * e2[j] for j in range(PEER_TOPK)] for i in range(PEER_TOPK)])
        z = best[0]
        for v in best[1:]:
            z = z + v
        rz = 1.0 / z
        e1n = [_round_bf16(v * rz) for v in e1]
        e2n = [_round_bf16(v) for v in e2]
        bestn = _top_of_lists([[_round_bf16(e1n[i] * e2n[j]) for j in range(PEER_TOPK)]
                               for i in range(PEER_TOPK)])
        th_ref[0, :, lanes] = bestn[PEER_TOPK - 1]
        for h in range(PEER_HEADS):
            rows = slice(h * N_KEYS, (h + 1) * N_KEYS)
            e1_ref[0, h, :, lanes] = jnp.exp(s_ref[0, lg, rows, :] - m1[h:h + 1, :]) * rz[h:h + 1, :]
            e2_ref[0, h, :, lanes] = jnp.exp(s_ref[1, lg, rows, :] - m2[h:h + 1, :]).astype(BF16)
        return carry

    lax.fori_loop(0, tm // LANES, lane_group, 0)


def _route(x2, g, wqt, keys_bf, *, tm):
    t, d = x2.shape
    hk = PEER_HEADS * N_KEYS
    n_groups = t // tm
    return pl.pallas_call(
        _route_body,
        grid=(t // tm,),
        in_specs=[
            pl.BlockSpec((tm, d), lambda i: (i, 0)),
            pl.BlockSpec((1, d), lambda i: (0, 0)),
            pl.BlockSpec(wqt.shape, lambda i: (0, 0)),
            pl.BlockSpec(keys_bf.shape, lambda i: (0, 0, 0)),
        ],
        out_specs=[
            pl.BlockSpec((1, d, tm), lambda i: (i, 0, 0)),
            pl.BlockSpec((1, PEER_HEADS, N_KEYS, tm), lambda i: (i, 0, 0, 0)),
            pl.BlockSpec((1, PEER_HEADS, N_KEYS, tm), lambda i: (i, 0, 0, 0)),
            pl.BlockSpec((1, PEER_HEADS, tm), lambda i: (i, 0, 0)),
        ],
        out_shape=[
            jax.ShapeDtypeStruct((n_groups, d, tm), BF16),
            jax.ShapeDtypeStruct((n_groups, PEER_HEADS, N_KEYS, tm), F32),
            jax.ShapeDtypeStruct((n_groups, PEER_HEADS, N_KEYS, tm), BF16),
            jax.ShapeDtypeStruct((n_groups, PEER_HEADS, tm), F32),
        ],
        scratch_shapes=[pltpu.VMEM((2, tm // LANES, hk, LANES), F32)],
        compiler_params=pltpu.CompilerParams(
            dimension_semantics=("arbitrary",), vmem_limit_bytes=VMEM_LIMIT),
        name="peer_route",
    )(x2, g.reshape(1, d), wqt, keys_bf)


def _rows_bf16(row):
    tile = jnp.broadcast_to(row, (2 * SUBLANES, row.shape[1])).astype(BF16)
    return pltpu.repeat(tile, N_KEYS // (2 * SUBLANES), axis=0)


def _fold_rows_min(x, rows):
    while x.shape[0] > rows:
        half = x.shape[0] // 2
        x = jnp.minimum(x[:half], x[half:])
    return x


def _experts_body(xft_ref, u_ref, vt_ref, e1_ref, e2_ref, th_ref, x2_ref, gfin_ref,
                  y_ref, acc_ref, gate_ref, coef_even_ref, coef_odd_ref, *, n_tiles, final_norm):
    s = pl.program_id(0)
    last = pl.num_programs(0) - 1
    n_groups, d, tg = xft_ref.shape
    te = u_ref.shape[0]
    keys_per_step = te // N_KEYS
    j = s % n_tiles
    key0 = j * keys_per_step
    slab_rows_per_key = d // keys_per_step
    assert n_tiles % 2 == 0

    def project_values(c, prev_ref, rows):
        step = min(ACC_SLAB, rows.stop - rows.start)
        for r in range(rows.start, rows.stop, step):
            acc_ref[c, r:r + step, :] += jnp.dot(vt_ref[r:r + step, :], prev_ref[c],
                                                 preferred_element_type=F32)

    def route_weights(c, k):
        rows = slice(k * N_KEYS, (k + 1) * N_KEYS)
        e1_rows = [e1_ref[c, h, pl.ds(key0 + k, 1), :] for h in range(PEER_HEADS)]
        low = None
        for lg in range(tg // LANES):
            lanes = slice(lg * LANES, (lg + 1) * LANES)
            gate = None
            for h in range(PEER_HEADS):
                prod = _rows_bf16(e1_rows[h][:, lanes]) * e2_ref[c, h, :, lanes]
                sel = jnp.where(prod >= _rows_bf16(th_ref[c, h:h + 1, lanes]), prod, 0.0)
                gate = sel if gate is None else gate + sel
            gate_ref[c, rows, lanes] = gate
            low = gate if low is None else jnp.minimum(low, gate)
        return _fold_rows_min(low, 2 * SUBLANES)

    def weighted_activations(c, next_ref):
        act = _gelu(jnp.dot(u_ref[...], xft_ref[c], preferred_element_type=F32))
        next_ref[c] = gate_ref[c] * act.astype(BF16)

    def steady(prev_ref, next_ref):
        for c in range(n_groups):
            for k in range(keys_per_step):
                low = route_weights(c, k)
                prev_ref[c, 0:2 * SUBLANES, 0:LANES] += jnp.minimum(low, 0.0)
                project_values(c, prev_ref, slice(k * slab_rows_per_key, (k + 1) * slab_rows_per_key))
            weighted_activations(c, next_ref)

    @pl.when(jnp.logical_and(j != 0, j % 2 == 0))
    def _():
        steady(coef_odd_ref, coef_even_ref)

    @pl.when(j % 2 == 1)
    def _():
        steady(coef_even_ref, coef_odd_ref)

    @pl.when(jnp.logical_and(j == 0, s > 0))
    def _():
        for c in range(n_groups):
            project_values(c, coef_odd_ref, slice(0, d))
        for c in range(n_groups):
            rows = slice(c * tg, (c + 1) * tg)
            x3 = x2_ref[rows, :] + acc_ref[c].T
            y_ref[rows, :] = _rms_norm(x3, gfin_ref[...]) if final_norm else x3

    @pl.when(jnp.logical_and(j == 0, s < last))
    def _():
        acc_ref[...] = jnp.zeros(acc_ref.shape, F32)
        for c in range(n_groups):
            for k in range(keys_per_step):
                route_weights(c, k)
            weighted_activations(c, coef_even_ref)


def _experts(xft, u_bf, vt_bf, e1, e2, th, x2, gfin, *, groups_per_tile, te, final_norm):
    t, d = x2.shape
    tg = xft.shape[2]
    gpt = groups_per_tile
    tm = gpt * tg
    n_exp = u_bf.shape[0]
    n_tiles = n_exp // te
    n_tok = t // tm
    n_steps = n_tok * n_tiles + 1

    def tok_now(s):
        return jnp.minimum(s // n_tiles, n_tok - 1)

    def tok_prev(s):
        return jnp.maximum(s - 1, 0) // n_tiles

    return pl.pallas_call(
        functools.partial(_experts_body, n_tiles=n_tiles, final_norm=final_norm),
        grid=(n_steps,),
        in_specs=[
            pl.BlockSpec((gpt, d, tg), lambda s: (tok_now(s), 0, 0)),
            pl.BlockSpec((te, d), lambda s: (s % n_tiles, 0)),
            pl.BlockSpec((d, te), lambda s: (0, jnp.maximum(s - 1, 0) % n_tiles)),
            pl.BlockSpec((gpt, PEER_HEADS, N_KEYS, tg), lambda s: (tok_now(s), 0, 0, 0)),
            pl.BlockSpec((gpt, PEER_HEADS, N_KEYS, tg), lambda s: (tok_now(s), 0, 0, 0)),
            pl.BlockSpec((gpt, PEER_HEADS, tg), lambda s: (tok_now(s), 0, 0)),
            pl.BlockSpec((tm, d), lambda s: (tok_prev(s), 0)),
            pl.BlockSpec((1, d), lambda s: (0, 0)),
        ],
        out_specs=pl.BlockSpec((tm, d), lambda s: (tok_prev(s), 0)),
        out_shape=jax.ShapeDtypeStruct((t, d), F32),
        scratch_shapes=[pltpu.VMEM((gpt, d, tg), F32), pltpu.VMEM((gpt, te, tg), BF16),
                        pltpu.VMEM((gpt, te, tg), BF16), pltpu.VMEM((gpt, te, tg), BF16)],
        compiler_params=pltpu.CompilerParams(
            dimension_semantics=("arbitrary",), vmem_limit_bytes=VMEM_LIMIT),
        name="peer_experts",
    )(xft, u_bf, vt_bf, e1, e2, th, x2, gfin.reshape(1, d))


def _pick_tile(n, pref):
    t = pref
    while n % t:
        t //= 2
    return t


def kernel(x_prompt, x_sample, state_conv, state_lru, g_mix, w_in, gmlp_ws, gmlp_bs, conv_w, conv_b,
           lru_wa, lru_ba, lru_wx, lru_bx, lru_lambda, w_out, g_ffn, peer_wq, peer_keys, peer_u,
           peer_v, g_final):
    batch, seq, d = x_prompt.shape
    dec_batch, dec_seq, _ = x_sample.shape
    depth = w_in.shape[0]
    assert batch == 1 and seq % GMLP_CHUNK == 0 and dec_seq <= CAUSAL_CHUNK
    assert d // LANES == gmlp_ws.shape[1] == lru_wa.shape[1]
    tp, ts = batch * seq, dec_batch * dec_seq

    idx = jnp.arange(GMLP_CHUNK)
    mask = (idx[:, None] // CAUSAL_CHUNK) >= (idx[None, :] // CAUSAL_CHUNK)
    hk = PEER_HEADS * N_KEYS

    xp = x_prompt.reshape(tp, d)
    xs = x_sample.reshape(ts, d)
    conv_p, lru_p, conv_s, lru_s, vrows_s = [], [], [], [], []
    for l in range(depth):
        last = l == depth - 1
        w_in_bf = w_in[l].astype(BF16)
        ws_p = (gmlp_ws[l] * mask.astype(F32)).astype(BF16)
        ws_s = ws_p[:, :dec_seq, :dec_seq]
        bias = jnp.repeat(gmlp_bs[l].T, LANES, axis=1)
        shared = (conv_w[l], conv_b[l].reshape(1, d), lru_wa[l].astype(BF16), lru_ba[l].reshape(1, d),
                  lru_wx[l].astype(BF16), lru_bx[l].reshape(1, d), lru_lambda[l].reshape(1, d),
                  w_out[l].astype(BF16))
        wqt = peer_wq[l].reshape(d, PEER_HEADS, 2, N_KEYS).transpose(2, 1, 3, 0).reshape(2 * hk, d)
        wqt = wqt.astype(BF16)
        keys_bf = peer_keys[l].astype(BF16)
        u_bf = peer_u[l].astype(BF16)
        vt_bf = peer_v[l].T.astype(BF16)
        gfin = g_final if last else jnp.ones((d,), F32)

        def ffn(x2):
            t = x2.shape[0]
            tm, tg = _pick_tile(t, 512), _pick_tile(t, 256)
            xft, e1, e2, th = _route(x2, g_ffn[l], wqt, keys_bf, tm=tg)
            return _experts(xft, u_bf, vt_bf, e1, e2, th, x2, gfin, groups_per_tile=tm // tg,
                            te=1024, final_norm=last)

        zp = _inproj(_prenorm(xp, g_mix[l], tm=_pick_tile(tp, 1024)), w_in_bf,
                     tm=_pick_tile(tp, 1024))
        zs = _inproj(_prenorm(xs, g_mix[l], tm=_pick_tile(ts, 1024)), w_in_bf,
                     tm=_pick_tile(ts, 1024))
        x2p, cp, lp = _mixer(zp, xp, 0, tp // GMLP_CHUNK, GMLP_CHUNK, ws_p, bias, *shared,
                             jnp.zeros((1, CONV_W - 1, d), F32), jnp.zeros((1, 1, d), F32),
                             carried=True)
        x2s, cs, ls = _mixer(zs, xs, 0, dec_batch, dec_seq, ws_s, bias[:dec_seq], *shared,
                             state_conv[l], state_lru[l].reshape(dec_batch, 1, d), carried=False)
        xp, xs = ffn(x2p), ffn(x2s)

        conv_p.append(cp)
        lru_p.append(lp.reshape(batch, d))
        conv_s.append(cs)
        lru_s.append(ls.reshape(dec_batch, d))
        vrows_s.append(zs[:, d:2 * d].reshape(dec_batch, dec_seq, d))

    return (xp.reshape(batch, seq, d), xs.reshape(dec_batch, dec_seq, d), jnp.stack(conv_p),
            jnp.stack(lru_p), jnp.stack(conv_s), jnp.stack(lru_s), jnp.stack(vrows_s))
```

```python
import functools
import math

import jax
import jax.numpy as jnp
from jax import lax
from jax.experimental import pallas as pl
from jax.experimental.pallas import tpu as pltpu

F32 = jnp.float32
BF16 = jnp.bfloat16

NORM_EPS = 1e-6
N_PARTS = 6
LANES = 128
SUBLANES = 8
CONV_W = 4
LRU_C = 8.0
GMLP_CHUNK = 128
CAUSAL_CHUNK = 64
N_KEYS = 128
PEER_HEADS = 8
PEER_TOPK = 16
VMEM_LIMIT = 56 * 1024 * 1024
ACC_SLAB = 256
INPROJ_SLAB = 256


def _gelu(x):
    c = math.sqrt(2.0 / math.pi)
    return x * (0.5 * (1.0 + jnp.tanh(c * (x + 0.044715 * (x * x * x)))))


def _rms_norm(x, g):
    ms = jnp.mean(x * x, axis=-1, keepdims=True)
    return (x * lax.rsqrt(ms + NORM_EPS)) * g


def _prenorm_body(x_ref, g_ref, xn_ref):
    xn_ref[...] = _rms_norm(x_ref[...], g_ref[...]).astype(BF16)


def _prenorm(x, g, *, tm):
    t, d = x.shape
    return pl.pallas_call(
        _prenorm_body,
        grid=(t // tm,),
        in_specs=[pl.BlockSpec((tm, d), lambda i: (i, 0)), pl.BlockSpec((1, d), lambda i: (0, 0))],
        out_specs=pl.BlockSpec((tm, d), lambda i: (i, 0)),
        out_shape=jax.ShapeDtypeStruct((t, d), BF16),
        compiler_params=pltpu.CompilerParams(
            dimension_semantics=("arbitrary",), vmem_limit_bytes=VMEM_LIMIT),
        name="prenorm",
    )(x, g.reshape(1, d))


def _inproj_body(xn_ref, w_ref, z_ref):
    part = pl.program_id(0)

    def project(activation):
        for r in range(0, z_ref.shape[0], INPROJ_SLAB):
            rows = slice(r, r + INPROJ_SLAB)
            z_ref[rows, :] = activation(
                jnp.dot(xn_ref[rows, :], w_ref[...], preferred_element_type=F32))

    is_gelu = jnp.logical_or(part <= 1, part == 3)

    @pl.when(is_gelu)
    def _():
        project(_gelu)

    @pl.when(part == 2)
    def _():
        project(lambda z: z)

    @pl.when(part >= 4)
    def _():
        project(jax.nn.sigmoid)


def _inproj(xn, w_bf, *, tm):
    t, d = xn.shape
    n = w_bf.shape[1]
    tn = n // N_PARTS
    return pl.pallas_call(
        _inproj_body,
        grid=(N_PARTS, t // tm),
        in_specs=[
            pl.BlockSpec((tm, d), lambda p, i: (i, 0)),
            pl.BlockSpec((d, tn), lambda p, i: (0, p)),
        ],
        out_specs=pl.BlockSpec((tm, tn), lambda p, i: (i, p)),
        out_shape=jax.ShapeDtypeStruct((t, n), F32),
        compiler_params=pltpu.CompilerParams(
            dimension_semantics=("arbitrary", "arbitrary"), vmem_limit_bytes=VMEM_LIMIT),
        name="inproj",
    )(xn, w_bf)


def _scan_rows(a, b, h_in):
    sub = lax.broadcasted_iota(jnp.int32, (SUBLANES, a.shape[1]), 0)
    keep = {s: sub >= s for s in (1, 2, 4)}
    blocks = []
    carry = h_in
    for r in range(0, a.shape[0], SUBLANES):
        ab, bb = a[r:r + SUBLANES, :], b[r:r + SUBLANES, :]
        for s in (1, 2, 4):
            a_sh = jnp.where(keep[s], pltpu.roll(ab, s, axis=0), 1.0)
            b_sh = jnp.where(keep[s], pltpu.roll(bb, s, axis=0), 0.0)
            bb = ab * b_sh + bb
            ab = ab * a_sh
        hb = ab * carry + bb
        carry = hb[SUBLANES - 1:SUBLANES, :]
        blocks.append(hb)
    return jnp.concatenate(blocks, axis=0), carry


def _mixer_body(zu_ref, zv_ref, zx_ref, zg_ref, zga_ref, zgb_ref, x_ref,
                wg_ref, bg_ref, cw_ref, cb_ref, wa_ref, ba_ref, wx_ref, bx_ref, lam_ref,
                wout_ref, prev_ref, h0_ref,
                x2_ref, conv_ref, hlast_ref,
                xp_ref, merged_ref, hc_ref, *, carried):
    c = pl.program_id(0)
    rows_c = zx_ref.shape[0]
    pad = SUBLANES
    n_groups = zx_ref.shape[1] // LANES

    if carried:
        @pl.when(c == 0)
        def _():
            xp_ref[0:pad, :] = jnp.zeros((pad, xp_ref.shape[1]), F32)
            hc_ref[...] = jnp.zeros(hc_ref.shape, F32)
    else:
        xp_ref[pad - (CONV_W - 1):pad, :] = prev_ref[0]
        hc_ref[0:1, :] = h0_ref[0]
    xp_ref[pad:pad + rows_c, :] = zx_ref[...]

    def group(j, carry):
        col = pl.ds(pl.multiple_of(j * LANES, LANES), LANES)
        s = jnp.dot(wg_ref[j], zv_ref[:, col].astype(BF16), preferred_element_type=F32) + bg_ref[:, col]
        y_a = zu_ref[:, col] * s
        xc = cb_ref[:, col]
        for k in range(CONV_W):
            lo = pad - (CONV_W - 1) + k
            xc = xc + xp_ref[lo:lo + rows_c, col] * cw_ref[k:k + 1, col]
        xcb = xc.astype(BF16)
        r = jax.nn.sigmoid(jnp.dot(xcb, wa_ref[j], preferred_element_type=F32) + ba_ref[:, col])
        i = jax.nn.sigmoid(jnp.dot(xcb, wx_ref[j], preferred_element_type=F32) + bx_ref[:, col])
        nlam = -lam_ref[:, col]
        softplus = jnp.maximum(nlam, 0.0) + jnp.log1p(jnp.exp(-jnp.abs(nlam)))
        log_a = (-LRU_C) * r * softplus
        a = jnp.exp(log_a)
        mult = jnp.sqrt(jnp.tanh(-log_a) * (a * a + 1.0))
        if carried:
            rows = lax.broadcasted_iota(jnp.int32, mult.shape, 0)
            mult = jnp.where(jnp.logical_and(rows == 0, c == 0), 1.0, mult)
        b = mult * i * xc
        h, h_last = _scan_rows(a, b, hc_ref[0:1, col])
        hc_ref[0:1, col] = h_last
        y_b = h * zg_ref[:, col]
        merged_ref[:, col] = (zga_ref[:, col] * y_a + zgb_ref[:, col] * y_b).astype(BF16)
        return carry

    lax.fori_loop(0, n_groups, group, 0, unroll=4)

    x2_ref[...] = x_ref[...] + jnp.dot(merged_ref[...], wout_ref[...], preferred_element_type=F32)
    conv_ref[0] = xp_ref[pad + rows_c - (CONV_W - 1):pad + rows_c, :]
    hlast_ref[0] = hc_ref[0:1, :]
    if carried:
        xp_ref[0:pad, :] = xp_ref[rows_c:rows_c + pad, :]


def _mixer(z, x, row0, n_seq, rows_c, wg, bg, cw, cb, wa, ba, wx, bx, lam, wout, prev, h0, *, carried):
    d = x.shape[1]
    blk0 = row0 // rows_c
    n_state = prev.shape[0]

    def zspec(part):
        return pl.BlockSpec((rows_c, d), lambda c, part=part: (blk0 + c, part))

    def const(shape):
        return pl.BlockSpec(shape, lambda c: (0,) * len(shape))

    def state_idx(c):
        return (c, 0, 0) if not carried else (0, 0, 0)

    return pl.pallas_call(
        functools.partial(_mixer_body, carried=carried),
        grid=(n_seq,),
        in_specs=[zspec(p) for p in range(N_PARTS)] + [
            pl.BlockSpec((rows_c, d), lambda c: (blk0 + c, 0)),
            const(wg.shape), const(bg.shape), const(cw.shape), const(cb.shape),
            const(wa.shape), const(ba.shape), const(wx.shape), const(bx.shape), const(lam.shape),
            const(wout.shape),
            pl.BlockSpec((1, CONV_W - 1, d), state_idx),
            pl.BlockSpec((1, 1, d), state_idx),
        ],
        out_specs=[
            pl.BlockSpec((rows_c, d), lambda c: (c, 0)),
            pl.BlockSpec((1, CONV_W - 1, d), state_idx),
            pl.BlockSpec((1, 1, d), state_idx),
        ],
        out_shape=[
            jax.ShapeDtypeStruct((n_seq * rows_c, d), F32),
            jax.ShapeDtypeStruct((n_state, CONV_W - 1, d), F32),
            jax.ShapeDtypeStruct((n_state, 1, d), F32),
        ],
        scratch_shapes=[
            pltpu.VMEM((rows_c + 2 * SUBLANES, d), F32),
            pltpu.VMEM((rows_c, d), BF16),
            pltpu.VMEM((SUBLANES, d), F32),
        ],
        compiler_params=pltpu.CompilerParams(
            dimension_semantics=("arbitrary",), vmem_limit_bytes=VMEM_LIMIT),
        name="mixer_stream" if carried else "mixer_step",
    )(z, z, z, z, z, z, x, wg, bg, cw, cb, wa, ba, wx, bx, lam, wout, prev, h0)


def _sort_network(n):
    pairs = []
    p = 1
    while p < n:
        k = p
        while k >= 1:
            for j in range(k % p, n - k, 2 * k):
                for i in range(min(k, n - j - k)):
                    if (i + j) // (2 * p) == (i + j + k) // (2 * p):
                        pairs.append((i + j, i + j + k))
            k //= 2
        p *= 2
    return pairs


_SORT16 = _sort_network(PEER_TOPK)


def _sort_desc(v):
    v = list(v)
    for i, j in _SORT16:
        v[i], v[j] = jnp.maximum(v[i], v[j]), jnp.minimum(v[i], v[j])
    return v


def _merge_top(xs, ys):
    z = [jnp.maximum(xs[i], ys[PEER_TOPK - 1 - i]) for i in range(PEER_TOPK)]
    d = PEER_TOPK // 2
    while d >= 1:
        for i in range(PEER_TOPK):
            if (i & d) == 0:
                z[i], z[i + d] = jnp.maximum(z[i], z[i + d]), jnp.minimum(z[i], z[i + d])
        d //= 2
    return z


def _top_of_lists(lists):
    while len(lists) > 1:
        lists = [_merge_top(lists[i], lists[i + 1]) for i in range(0, len(lists), 2)]
    return lists[0]


def _round_bf16(x):
    return x.astype(BF16).astype(F32)


def _route_body(x2_ref, g_ref, wqt_ref, keys_ref, xft_ref, e1_ref, e2_ref, th_ref, s_ref):
    tm = x2_ref.shape[0]
    hk = PEER_HEADS * N_KEYS
    xft = _rms_norm(x2_ref[...], g_ref[...]).T.astype(BF16)
    xft_ref[0] = xft
    qt = jnp.dot(wqt_ref[...], xft, preferred_element_type=F32)
    for p in range(2):
        for h in range(PEER_HEADS):
            lo = p * hk + h * N_KEYS
            q_hp = qt[lo:lo + N_KEYS, :].astype(BF16)
            s_hp = jnp.dot(keys_ref[p], q_hp, preferred_element_type=F32)
            for g in range(tm // LANES):
                s_ref[p, g, h * N_KEYS:(h + 1) * N_KEYS, :] = s_hp[:, g * LANES:(g + 1) * LANES]

    def lane_group(lg, carry):
        lanes = pl.ds(pl.multiple_of(lg * LANES, LANES), LANES)
        tops = []
        for p in range(2):
            vals = [s_ref[p, lg, pl.ds(n, PEER_HEADS, stride=N_KEYS), :] for n in range(N_KEYS)]
            groups = [_sort_desc(vals[i:i + PEER_TOPK]) for i in range(0, N_KEYS, PEER_TOPK)]
            tops.append(_top_of_lists(groups))
        m1, m2 = tops[0][0], tops[1][0]
        e1 = [jnp.exp(v - m1) for v in tops[0]]
        e2 = [jnp.exp(v - m2) for v in tops[1]]
        best = _top_of_lists([[e1[i] * e2[j] for j in range(PEER_TOPK)] for i in range(PEER_TOPK)])
        z = best[0]
        for v in best[1:]:
            z = z + v
        rz = 1.0 / z
        e1n = [_round_bf16(v * rz) for v in e1]
        e2n = [_round_bf16(v) for v in e2]
        bestn = _top_of_lists([[_round_bf16(e1n[i] * e2n[j]) for j in range(PEER_TOPK)]
                               for i in range(PEER_TOPK)])
        th_ref[0, :, lanes] = bestn[PEER_TOPK - 1]
        for h in range(PEER_HEADS):
            rows = slice(h * N_KEYS, (h + 1) * N_KEYS)
            e1_ref[0, h, :, lanes] = jnp.exp(s_ref[0, lg, rows, :] - m1[h:h + 1, :]) * rz[h:h + 1, :]
            e2_ref[0, h, :, lanes] = jnp.exp(s_ref[1, lg, rows, :] - m2[h:h + 1, :]).astype(BF16)
        return carry

    lax.fori_loop(0, tm // LANES, lane_group, 0)


def _route(x2, g, wqt, keys_bf, *, tm):
    t, d = x2.shape
    hk = PEER_HEADS * N_KEYS
    n_groups = t // tm
    return pl.pallas_call(
        _route_body,
        grid=(t // tm,),
        in_specs=[
            pl.BlockSpec((tm, d), lambda i: (i, 0)),
            pl.BlockSpec((1, d), lambda i: (0, 0)),
            pl.BlockSpec(wqt.shape, lambda i: (0, 0)),
            pl.BlockSpec(keys_bf.shape, lambda i: (0, 0, 0)),
        ],
        out_specs=[
            pl.BlockSpec((1, d, tm), lambda i: (i, 0, 0)),
            pl.BlockSpec((1, PEER_HEADS, N_KEYS, tm), lambda i: (i, 0, 0, 0)),
            pl.BlockSpec((1, PEER_HEADS, N_KEYS, tm), lambda i: (i, 0, 0, 0)),
            pl.BlockSpec((1, PEER_HEADS, tm), lambda i: (i, 0, 0)),
        ],
        out_shape=[
            jax.ShapeDtypeStruct((n_groups, d, tm), BF16),
            jax.ShapeDtypeStruct((n_groups, PEER_HEADS, N_KEYS, tm), F32),
            jax.ShapeDtypeStruct((n_groups, PEER_HEADS, N_KEYS, tm), BF16),
            jax.ShapeDtypeStruct((n_groups, PEER_HEADS, tm), F32),
        ],
        scratch_shapes=[pltpu.VMEM((2, tm // LANES, hk, LANES), F32)],
        compiler_params=pltpu.CompilerParams(
            dimension_semantics=("arbitrary",), vmem_limit_bytes=VMEM_LIMIT),
        name="peer_route",
    )(x2, g.reshape(1, d), wqt, keys_bf)


def _rows_bf16(row):
    tile = jnp.broadcast_to(row, (2 * SUBLANES, row.shape[1])).astype(BF16)
    return pltpu.repeat(tile, N_KEYS // (2 * SUBLANES), axis=0)


def _fold_rows_min(x, rows):
    while x.shape[0] > rows:
        half = x.shape[0] // 2
        x = jnp.minimum(x[:half], x[half:])
    return x


def _experts_body(xft_ref, u_ref, vt_ref, e1_ref, e2_ref, th_ref, x2_ref, gfin_ref,
                  y_ref, acc_ref, gate_ref, coef_even_ref, coef_odd_ref, *, n_tiles, final_norm):
    s = pl.program_id(0)
    last = pl.num_programs(0) - 1
    n_groups, d, tg = xft_ref.shape
    te = u_ref.shape[0]
    keys_per_step = te // N_KEYS
    j = s % n_tiles
    key0 = j * keys_per_step
    slab_rows_per_key = d // keys_per_step
    assert n_tiles % 2 == 0

    def project_values(c, prev_ref, rows):
        step = min(ACC_SLAB, rows.stop - rows.start)
        for r in range(rows.start, rows.stop, step):
            acc_ref[c, r:r + step, :] += jnp.dot(vt_ref[r:r + step, :], prev_ref[c],
                                                 preferred_element_type=F32)

    def route_weights(c, k):
        rows = slice(k * N_KEYS, (k + 1) * N_KEYS)
        e1_rows = [e1_ref[c, h, pl.ds(key0 + k, 1), :] for h in range(PEER_HEADS)]
        low = None
        for lg in range(tg // LANES):
            lanes = slice(lg * LANES, (lg + 1) * LANES)
            gate = None
            for h in range(PEER_HEADS):
                prod = _rows_bf16(e1_rows[h][:, lanes]) * e2_ref[c, h, :, lanes]
                sel = jnp.where(prod >= _rows_bf16(th_ref[c, h:h + 1, lanes]), prod, 0.0)
                gate = sel if gate is None else gate + sel
            gate_ref[c, rows, lanes] = gate
            low = gate if low is None else jnp.minimum(low, gate)
        return _fold_rows_min(low, 2 * SUBLANES)

    def activations(c):
        return _gelu(jnp.dot(u_ref[...], xft_ref[c], preferred_element_type=F32)).astype(BF16)

    def steady(prev_ref, next_ref):
        keys_per_slab = max(1, ACC_SLAB // slab_rows_per_key)
        for c in range(n_groups):
            act = activations(c)
            for k0 in range(0, keys_per_step, keys_per_slab):
                low = route_weights(c, k0)
                for k in range(k0 + 1, k0 + keys_per_slab):
                    low = jnp.minimum(low, route_weights(c, k))
                if c or k0:
                    prev_ref[c, 0:2 * SUBLANES, 0:LANES] += jnp.minimum(low, 0.0)
                project_values(c, prev_ref, slice(k0 * slab_rows_per_key,
                                                  (k0 + keys_per_slab) * slab_rows_per_key))
            next_ref[c] = gate_ref[c] * act

    @pl.when(jnp.logical_and(j != 0, j % 2 == 0))
    def _():
        steady(coef_odd_ref, coef_even_ref)

    @pl.when(j % 2 == 1)
    def _():
        steady(coef_even_ref, coef_odd_ref)

    @pl.when(jnp.logical_and(j == 0, s > 0))
    def _():
        for c in range(n_groups):
            project_values(c, coef_odd_ref, slice(0, d))
        for c in range(n_groups):
            rows = slice(c * tg, (c + 1) * tg)
            x3 = x2_ref[rows, :] + acc_ref[c].T
            y_ref[rows, :] = _rms_norm(x3, gfin_ref[...]) if final_norm else x3

    @pl.when(jnp.logical_and(j == 0, s < last))
    def _():
        acc_ref[...] = jnp.zeros(acc_ref.shape, F32)
        for c in range(n_groups):
            for k in range(keys_per_step):
                route_weights(c, k)
            coef_even_ref[c] = gate_ref[c] * activations(c)


def _experts(xft, u_bf, vt_bf, e1, e2, th, x2, gfin, *, groups_per_tile, te, final_norm):
    t, d = x2.shape
    tg = xft.shape[2]
    gpt = groups_per_tile
    tm = gpt * tg
    n_exp = u_bf.shape[0]
    n_tiles = n_exp // te
    n_tok = t // tm
    n_steps = n_tok * n_tiles + 1

    def tok_now(s):
        return jnp.minimum(s // n_tiles, n_tok - 1)

    def tok_prev(s):
        return jnp.maximum(s - 1, 0) // n_tiles

    return pl.pallas_call(
        functools.partial(_experts_body, n_tiles=n_tiles, final_norm=final_norm),
        grid=(n_steps,),
        in_specs=[
            pl.BlockSpec((gpt, d, tg), lambda s: (tok_now(s), 0, 0)),
            pl.BlockSpec((te, d), lambda s: (s % n_tiles, 0)),
            pl.BlockSpec((d, te), lambda s: (0, jnp.maximum(s - 1, 0) % n_tiles)),
            pl.BlockSpec((gpt, PEER_HEADS, N_KEYS, tg), lambda s: (tok_now(s), 0, 0, 0)),
            pl.BlockSpec((gpt, PEER_HEADS, N_KEYS, tg), lambda s: (tok_now(s), 0, 0, 0)),
            pl.BlockSpec((gpt, PEER_HEADS, tg), lambda s: (tok_now(s), 0, 0)),
            pl.BlockSpec((tm, d), lambda s: (tok_prev(s), 0)),
            pl.BlockSpec((1, d), lambda s: (0, 0)),
        ],
        out_specs=pl.BlockSpec((tm, d), lambda s: (tok_prev(s), 0)),
        out_shape=jax.ShapeDtypeStruct((t, d), F32),
        scratch_shapes=[pltpu.VMEM((gpt, d, tg), F32), pltpu.VMEM((gpt, te, tg), BF16),
                        pltpu.VMEM((gpt, te, tg), BF16), pltpu.VMEM((gpt, te, tg), BF16)],
        compiler_params=pltpu.CompilerParams(
            dimension_semantics=("arbitrary",), vmem_limit_bytes=VMEM_LIMIT),
        name="peer_experts",
    )(xft, u_bf, vt_bf, e1, e2, th, x2, gfin.reshape(1, d))


def _pick_tile(n, pref):
    t = pref
    while n % t:
        t //= 2
    return t


def kernel(x_prompt, x_sample, state_conv, state_lru, g_mix, w_in, gmlp_ws, gmlp_bs, conv_w, conv_b,
           lru_wa, lru_ba, lru_wx, lru_bx, lru_lambda, w_out, g_ffn, peer_wq, peer_keys, peer_u,
           peer_v, g_final):
    batch, seq, d = x_prompt.shape
    dec_batch, dec_seq, _ = x_sample.shape
    depth = w_in.shape[0]
    assert batch == 1 and seq % GMLP_CHUNK == 0 and dec_seq <= CAUSAL_CHUNK
    assert d // LANES == gmlp_ws.shape[1] == lru_wa.shape[1]
    tp, ts = batch * seq, dec_batch * dec_seq

    idx = jnp.arange(GMLP_CHUNK)
    mask = (idx[:, None] // CAUSAL_CHUNK) >= (idx[None, :] // CAUSAL_CHUNK)
    hk = PEER_HEADS * N_KEYS

    xp = x_prompt.reshape(tp, d)
    xs = x_sample.reshape(ts, d)
    conv_p, lru_p, conv_s, lru_s, vrows_s = [], [], [], [], []
    for l in range(depth):
        last = l == depth - 1
        w_in_bf = w_in[l].astype(BF16)
        ws_p = (gmlp_ws[l] * mask.astype(F32)).astype(BF16)
        ws_s = ws_p[:, :dec_seq, :dec_seq]
        bias = jnp.repeat(gmlp_bs[l].T, LANES, axis=1)
        shared = (conv_w[l], conv_b[l].reshape(1, d), lru_wa[l].astype(BF16), lru_ba[l].reshape(1, d),
                  lru_wx[l].astype(BF16), lru_bx[l].reshape(1, d), lru_lambda[l].reshape(1, d),
                  w_out[l].astype(BF16))
        wqt = peer_wq[l].reshape(d, PEER_HEADS, 2, N_KEYS).transpose(2, 1, 3, 0).reshape(2 * hk, d)
        wqt = wqt.astype(BF16)
        keys_bf = peer_keys[l].astype(BF16)
        u_bf = peer_u[l].astype(BF16)
        vt_bf = peer_v[l].T.astype(BF16)
        gfin = g_final if last else jnp.ones((d,), F32)

        def ffn(x2):
            t = x2.shape[0]
            tm, tg = _pick_tile(t, 512), _pick_tile(t, 256)
            xft, e1, e2, th = _route(x2, g_ffn[l], wqt, keys_bf, tm=tg)
            return _experts(xft, u_bf, vt_bf, e1, e2, th, x2, gfin, groups_per_tile=tm // tg,
                            te=1024, final_norm=last)

        zp = _inproj(_prenorm(xp, g_mix[l], tm=_pick_tile(tp, 1024)), w_in_bf,
                     tm=_pick_tile(tp, 1024))
        zs = _inproj(_prenorm(xs, g_mix[l], tm=_pick_tile(ts, 1024)), w_in_bf,
                     tm=_pick_tile(ts, 1024))
        x2p, cp, lp = _mixer(zp, xp, 0, tp // GMLP_CHUNK, GMLP_CHUNK, ws_p, bias, *shared,
                             jnp.zeros((1, CONV_W - 1, d), F32), jnp.zeros((1, 1, d), F32),
                             carried=True)
        x2s, cs, ls = _mixer(zs, xs, 0, dec_batch, dec_seq, ws_s, bias[:dec_seq], *shared,
                             state_conv[l], state_lru[l].reshape(dec_batch, 1, d), carried=False)
        xp, xs = ffn(x2p), ffn(x2s)

        conv_p.append(cp)
        lru_p.append(lp.reshape(batch, d))
        conv_s.append(cs)
        lru_s.append(ls.reshape(dec_batch, d))
        vrows_s.append(zs[:, d:2 * d].reshape(dec_batch, dec_seq, d))

    return (xp.reshape(batch, seq, d), xs.reshape(dec_batch, dec_seq, d), jnp.stack(conv_p),
            jnp.stack(lru_p), jnp.stack(conv_s), jnp.stack(lru_s), jnp.stack(vrows_s))
```

```python
import functools
import math

import jax
import jax.numpy as jnp
from jax import lax
from jax.experimental import pallas as pl
from jax.experimental.pallas import tpu as pltpu

F32 = jnp.float32
BF16 = jnp.bfloat16

NORM_EPS = 1e-6
N_PARTS = 6
LANES = 128
SUBLANES = 8
CONV_W = 4
LRU_C = 8.0
GMLP_CHUNK = 128
CAUSAL_CHUNK = 64
N_KEYS = 128
PEER_HEADS = 8
PEER_TOPK = 16
VMEM_LIMIT = 56 * 1024 * 1024
ACC_SLAB = 256
INPROJ_SLAB = 256


def _gelu(x):
    c = math.sqrt(2.0 / math.pi)
    return x * (0.5 * (1.0 + jnp.tanh(c * (x + 0.044715 * (x * x * x)))))


def _gelu_twice(x):
    c = math.sqrt(2.0 / math.pi)
    return x * (1.0 + jnp.tanh(x * (c + (0.044715 * c) * (x * x))))


def _rms_norm(x, g):
    ms = jnp.mean(x * x, axis=-1, keepdims=True)
    return (x * lax.rsqrt(ms + NORM_EPS)) * g


def _prenorm_body(x_ref, g_ref, xn_ref):
    xn_ref[...] = _rms_norm(x_ref[...], g_ref[...]).astype(BF16)


def _prenorm(x, g, *, tm):
    t, d = x.shape
    return pl.pallas_call(
        _prenorm_body,
        grid=(t // tm,),
        in_specs=[pl.BlockSpec((tm, d), lambda i: (i, 0)), pl.BlockSpec((1, d), lambda i: (0, 0))],
        out_specs=pl.BlockSpec((tm, d), lambda i: (i, 0)),
        out_shape=jax.ShapeDtypeStruct((t, d), BF16),
        compiler_params=pltpu.CompilerParams(
            dimension_semantics=("arbitrary",), vmem_limit_bytes=VMEM_LIMIT),
        name="prenorm",
    )(x, g.reshape(1, d))


def _inproj_body(xn_ref, w_ref, z_ref):
    part = pl.program_id(0)

    def project(activation):
        for r in range(0, z_ref.shape[0], INPROJ_SLAB):
            rows = slice(r, r + INPROJ_SLAB)
            z_ref[rows, :] = activation(
                jnp.dot(xn_ref[rows, :], w_ref[...], preferred_element_type=F32))

    is_gelu = jnp.logical_or(part <= 1, part == 3)

    @pl.when(is_gelu)
    def _():
        project(_gelu)

    @pl.when(part == 2)
    def _():
        project(lambda z: z)

    @pl.when(part >= 4)
    def _():
        project(jax.nn.sigmoid)


def _inproj(xn, w_bf, *, tm):
    t, d = xn.shape
    n = w_bf.shape[1]
    tn = n // N_PARTS
    return pl.pallas_call(
        _inproj_body,
        grid=(N_PARTS, t // tm),
        in_specs=[
            pl.BlockSpec((tm, d), lambda p, i: (i, 0)),
            pl.BlockSpec((d, tn), lambda p, i: (0, p)),
        ],
        out_specs=pl.BlockSpec((tm, tn), lambda p, i: (i, p)),
        out_shape=jax.ShapeDtypeStruct((t, n), F32),
        compiler_params=pltpu.CompilerParams(
            dimension_semantics=("arbitrary", "arbitrary"), vmem_limit_bytes=VMEM_LIMIT),
        name="inproj",
    )(xn, w_bf)


def _scan_rows(a, b, h_in):
    sub = lax.broadcasted_iota(jnp.int32, (SUBLANES, a.shape[1]), 0)
    keep = {s: sub >= s for s in (1, 2, 4)}
    blocks = []
    carry = h_in
    for r in range(0, a.shape[0], SUBLANES):
        ab, bb = a[r:r + SUBLANES, :], b[r:r + SUBLANES, :]
        for s in (1, 2, 4):
            a_sh = jnp.where(keep[s], pltpu.roll(ab, s, axis=0), 1.0)
            b_sh = jnp.where(keep[s], pltpu.roll(bb, s, axis=0), 0.0)
            bb = ab * b_sh + bb
            ab = ab * a_sh
        hb = ab * carry + bb
        carry = hb[SUBLANES - 1:SUBLANES, :]
        blocks.append(hb)
    return jnp.concatenate(blocks, axis=0), carry


def _mixer_body(zu_ref, zv_ref, zx_ref, zg_ref, zga_ref, zgb_ref, x_ref,
                wg_ref, bg_ref, cw_ref, cb_ref, wa_ref, ba_ref, wx_ref, bx_ref, lam_ref,
                wout_ref, prev_ref, h0_ref,
                x2_ref, conv_ref, hlast_ref,
                xp_ref, merged_ref, hc_ref, *, carried):
    c = pl.program_id(0)
    rows_c = zx_ref.shape[0]
    pad = SUBLANES
    n_groups = zx_ref.shape[1] // LANES

    if carried:
        @pl.when(c == 0)
        def _():
            xp_ref[0:pad, :] = jnp.zeros((pad, xp_ref.shape[1]), F32)
            hc_ref[...] = jnp.zeros(hc_ref.shape, F32)
    else:
        xp_ref[pad - (CONV_W - 1):pad, :] = prev_ref[0]
        hc_ref[0:1, :] = h0_ref[0]
    xp_ref[pad:pad + rows_c, :] = zx_ref[...]

    def group(j, carry):
        col = pl.ds(pl.multiple_of(j * LANES, LANES), LANES)
        s = jnp.dot(wg_ref[j], zv_ref[:, col].astype(BF16), preferred_element_type=F32) + bg_ref[:, col]
        y_a = zu_ref[:, col] * s
        xc = cb_ref[:, col]
        for k in range(CONV_W):
            lo = pad - (CONV_W - 1) + k
            xc = xc + xp_ref[lo:lo + rows_c, col] * cw_ref[k:k + 1, col]
        xcb = xc.astype(BF16)
        r = jax.nn.sigmoid(jnp.dot(xcb, wa_ref[j], preferred_element_type=F32) + ba_ref[:, col])
        i = jax.nn.sigmoid(jnp.dot(xcb, wx_ref[j], preferred_element_type=F32) + bx_ref[:, col])
        nlam = -lam_ref[:, col]
        softplus = jnp.maximum(nlam, 0.0) + jnp.log1p(jnp.exp(-jnp.abs(nlam)))
        log_a = (-LRU_C) * r * softplus
        a = jnp.exp(log_a)
        mult = jnp.sqrt(jnp.tanh(-log_a) * (a * a + 1.0))
        if carried:
            rows = lax.broadcasted_iota(jnp.int32, mult.shape, 0)
            mult = jnp.where(jnp.logical_and(rows == 0, c == 0), 1.0, mult)
        b = mult * i * xc
        h, h_last = _scan_rows(a, b, hc_ref[0:1, col])
        hc_ref[0:1, col] = h_last
        y_b = h * zg_ref[:, col]
        merged_ref[:, col] = (zga_ref[:, col] * y_a + zgb_ref[:, col] * y_b).astype(BF16)
        return carry

    lax.fori_loop(0, n_groups, group, 0, unroll=4)

    x2_ref[...] = x_ref[...] + jnp.dot(merged_ref[...], wout_ref[...], preferred_element_type=F32)
    conv_ref[0] = xp_ref[pad + rows_c - (CONV_W - 1):pad + rows_c, :]
    hlast_ref[0] = hc_ref[0:1, :]
    if carried:
        xp_ref[0:pad, :] = xp_ref[rows_c:rows_c + pad, :]


def _mixer(z, x, row0, n_seq, rows_c, wg, bg, cw, cb, wa, ba, wx, bx, lam, wout, prev, h0, *, carried):
    d = x.shape[1]
    blk0 = row0 // rows_c
    n_state = prev.shape[0]

    def zspec(part):
        return pl.BlockSpec((rows_c, d), lambda c, part=part: (blk0 + c, part))

    def const(shape):
        return pl.BlockSpec(shape, lambda c: (0,) * len(shape))

    def state_idx(c):
        return (c, 0, 0) if not carried else (0, 0, 0)

    return pl.pallas_call(
        functools.partial(_mixer_body, carried=carried),
        grid=(n_seq,),
        in_specs=[zspec(p) for p in range(N_PARTS)] + [
            pl.BlockSpec((rows_c, d), lambda c: (blk0 + c, 0)),
            const(wg.shape), const(bg.shape), const(cw.shape), const(cb.shape),
            const(wa.shape), const(ba.shape), const(wx.shape), const(bx.shape), const(lam.shape),
            const(wout.shape),
            pl.BlockSpec((1, CONV_W - 1, d), state_idx),
            pl.BlockSpec((1, 1, d), state_idx),
        ],
        out_specs=[
            pl.BlockSpec((rows_c, d), lambda c: (c, 0)),
            pl.BlockSpec((1, CONV_W - 1, d), state_idx),
            pl.BlockSpec((1, 1, d), state_idx),
        ],
        out_shape=[
            jax.ShapeDtypeStruct((n_seq * rows_c, d), F32),
            jax.ShapeDtypeStruct((n_state, CONV_W - 1, d), F32),
            jax.ShapeDtypeStruct((n_state, 1, d), F32),
        ],
        scratch_shapes=[
            pltpu.VMEM((rows_c + 2 * SUBLANES, d), F32),
            pltpu.VMEM((rows_c, d), BF16),
            pltpu.VMEM((SUBLANES, d), F32),
        ],
        compiler_params=pltpu.CompilerParams(
            dimension_semantics=("arbitrary",), vmem_limit_bytes=VMEM_LIMIT),
        name="mixer_stream" if carried else "mixer_step",
    )(z, z, z, z, z, z, x, wg, bg, cw, cb, wa, ba, wx, bx, lam, wout, prev, h0)


def _sort_network(n):
    pairs = []
    p = 1
    while p < n:
        k = p
        while k >= 1:
            for j in range(k % p, n - k, 2 * k):
                for i in range(min(k, n - j - k)):
                    if (i + j) // (2 * p) == (i + j + k) // (2 * p):
                        pairs.append((i + j, i + j + k))
            k //= 2
        p *= 2
    return pairs


_SORT16 = _sort_network(PEER_TOPK)


def _sort_desc(v):
    v = list(v)
    for i, j in _SORT16:
        v[i], v[j] = jnp.maximum(v[i], v[j]), jnp.minimum(v[i], v[j])
    return v


def _merge_top(xs, ys):
    z = [jnp.maximum(xs[i], ys[PEER_TOPK - 1 - i]) for i in range(PEER_TOPK)]
    d = PEER_TOPK // 2
    while d >= 1:
        for i in range(PEER_TOPK):
            if (i & d) == 0:
                z[i], z[i + d] = jnp.maximum(z[i], z[i + d]), jnp.minimum(z[i], z[i + d])
        d //= 2
    return z


def _top_of_lists(lists):
    while len(lists) > 1:
        lists = [_merge_top(lists[i], lists[i + 1]) for i in range(0, len(lists), 2)]
    return lists[0]


def _round_bf16(x):
    return x.astype(BF16).astype(F32)


def _route_body(x2_ref, g_ref, wqt_ref, keys_ref, xft_ref, e1_ref, e2_ref, th_ref, s_ref):
    tm = x2_ref.shape[0]
    hk = PEER_HEADS * N_KEYS
    xft = _rms_norm(x2_ref[...], g_ref[...]).T.astype(BF16)
    xft_ref[0] = xft
    qt = jnp.dot(wqt_ref[...], xft, preferred_element_type=F32)
    for p in range(2):
        for h in range(PEER_HEADS):
            lo = p * hk + h * N_KEYS
            q_hp = qt[lo:lo + N_KEYS, :].astype(BF16)
            s_hp = jnp.dot(keys_ref[p], q_hp, preferred_element_type=F32)
            for g in range(tm // LANES):
                s_ref[p, g, h * N_KEYS:(h + 1) * N_KEYS, :] = s_hp[:, g * LANES:(g + 1) * LANES]

    def lane_group(lg, carry):
        lanes = pl.ds(pl.multiple_of(lg * LANES, LANES), LANES)
        tops = []
        for p in range(2):
            vals = [s_ref[p, lg, pl.ds(n, PEER_HEADS, stride=N_KEYS), :] for n in range(N_KEYS)]
            groups = [_sort_desc(vals[i:i + PEER_TOPK]) for i in range(0, N_KEYS, PEER_TOPK)]
            tops.append(_top_of_lists(groups))
        m1, m2 = tops[0][0], tops[1][0]
        e1 = [jnp.exp(v - m1) for v in tops[0]]
        e2 = [jnp.exp(v - m2) for v in tops[1]]
        best = _top_of_lists([[e1[i] * e2[j] for j in range(PEER_TOPK)] for i in range(PEER_TOPK)])
        z = best[0]
        for v in best[1:]:
            z = z + v
        rz = 0.5 / z
        e1n = [_round_bf16(v * rz) for v in e1]
        e2n = [_round_bf16(v) for v in e2]
        bestn = _top_of_lists([[_round_bf16(e1n[i] * e2n[j]) for j in range(PEER_TOPK)]
                               for i in range(PEER_TOPK)])
        th_ref[0, :, lanes] = bestn[PEER_TOPK - 1]
        for h in range(PEER_HEADS):
            rows = slice(h * N_KEYS, (h + 1) * N_KEYS)
            e1_ref[0, h, :, lanes] = jnp.exp(s_ref[0, lg, rows, :] - m1[h:h + 1, :]) * rz[h:h + 1, :]
            e2_ref[0, h, :, lanes] = jnp.exp(s_ref[1, lg, rows, :] - m2[h:h + 1, :]).astype(BF16)
        return carry

    lax.fori_loop(0, tm // LANES, lane_group, 0)


def _route(x2, g, wqt, keys_bf, *, tm):
    t, d = x2.shape
    hk = PEER_HEADS * N_KEYS
    n_groups = t // tm
    return pl.pallas_call(
        _route_body,
        grid=(t // tm,),
        in_specs=[
            pl.BlockSpec((tm, d), lambda i: (i, 0)),
            pl.BlockSpec((1, d), lambda i: (0, 0)),
            pl.BlockSpec(wqt.shape, lambda i: (0, 0)),
            pl.BlockSpec(keys_bf.shape, lambda i: (0, 0, 0)),
        ],
        out_specs=[
            pl.BlockSpec((1, d, tm), lambda i: (i, 0, 0)),
            pl.BlockSpec((1, PEER_HEADS, N_KEYS, tm), lambda i: (i, 0, 0, 0)),
            pl.BlockSpec((1, PEER_HEADS, N_KEYS, tm), lambda i: (i, 0, 0, 0)),
            pl.BlockSpec((1, PEER_HEADS, tm), lambda i: (i, 0, 0)),
        ],
        out_shape=[
            jax.ShapeDtypeStruct((n_groups, d, tm), BF16),
            jax.ShapeDtypeStruct((n_groups, PEER_HEADS, N_KEYS, tm), F32),
            jax.ShapeDtypeStruct((n_groups, PEER_HEADS, N_KEYS, tm), BF16),
            jax.ShapeDtypeStruct((n_groups, PEER_HEADS, tm), F32),
        ],
        scratch_shapes=[pltpu.VMEM((2, tm // LANES, hk, LANES), F32)],
        compiler_params=pltpu.CompilerParams(
            dimension_semantics=("arbitrary",), vmem_limit_bytes=VMEM_LIMIT),
        name="peer_route",
    )(x2, g.reshape(1, d), wqt, keys_bf)


def _rows_bf16(row):
    tile = jnp.broadcast_to(row, (2 * SUBLANES, row.shape[1])).astype(BF16)
    return pltpu.repeat(tile, N_KEYS // (2 * SUBLANES), axis=0)


def _fold_rows_min(x, rows):
    while x.shape[0] > rows:
        half = x.shape[0] // 2
        x = jnp.minimum(x[:half], x[half:])
    return x


def _experts_body(xft_ref, u_ref, vt_ref, e1_ref, e2_ref, th_ref, x2_ref, gfin_ref,
                  y_ref, acc_ref, gate_ref, coef_even_ref, coef_odd_ref, *, n_tiles, final_norm):
    s = pl.program_id(0)
    last = pl.num_programs(0) - 1
    n_groups, d, tg = xft_ref.shape
    te = u_ref.shape[0]
    keys_per_step = te // N_KEYS
    j = s % n_tiles
    key0 = j * keys_per_step
    slab_rows_per_key = d // keys_per_step
    assert n_tiles % 2 == 0

    def project_values(c, prev_ref, rows):
        step = min(ACC_SLAB, rows.stop - rows.start)
        for r in range(rows.start, rows.stop, step):
            acc_ref[c, r:r + step, :] += jnp.dot(vt_ref[r:r + step, :], prev_ref[c],
                                                 preferred_element_type=F32)

    def route_weights(c, k):
        rows = slice(k * N_KEYS, (k + 1) * N_KEYS)
        e1_rows = [e1_ref[c, h, pl.ds(key0 + k, 1), :] for h in range(PEER_HEADS)]
        low = None
        for lg in range(tg // LANES):
            lanes = slice(lg * LANES, (lg + 1) * LANES)
            gate = None
            for h in range(PEER_HEADS):
                prod = _rows_bf16(e1_rows[h][:, lanes]) * e2_ref[c, h, :, lanes]
                sel = jnp.where(prod >= _rows_bf16(th_ref[c, h:h + 1, lanes]), prod, 0.0)
                gate = sel if gate is None else gate + sel
            gate_ref[c, rows, lanes] = gate
            low = gate if low is None else jnp.minimum(low, gate)
        return _fold_rows_min(low, 2 * SUBLANES)

    def activations(c):
        x = jnp.dot(u_ref[...], xft_ref[c], preferred_element_type=F32)
        return _gelu_twice(x).astype(BF16)

    def steady(prev_ref, next_ref):
        keys_per_slab = max(1, ACC_SLAB // slab_rows_per_key)
        for c in range(n_groups):
            act = activations(c)
            for k0 in range(0, keys_per_step, keys_per_slab):
                low = route_weights(c, k0)
                for k in range(k0 + 1, k0 + keys_per_slab):
                    low = jnp.minimum(low, route_weights(c, k))
                if c or k0:
                    prev_ref[c, 0:2 * SUBLANES, 0:LANES] += jnp.minimum(low, 0.0)
                project_values(c, prev_ref, slice(k0 * slab_rows_per_key,
                                                  (k0 + keys_per_slab) * slab_rows_per_key))
            next_ref[c] = gate_ref[c] * act

    @pl.when(jnp.logical_and(j != 0, j % 2 == 0))
    def _():
        steady(coef_odd_ref, coef_even_ref)

    @pl.when(j % 2 == 1)
    def _():
        steady(coef_even_ref, coef_odd_ref)

    @pl.when(jnp.logical_and(j == 0, s > 0))
    def _():
        for c in range(n_groups):
            project_values(c, coef_odd_ref, slice(0, d))
        for c in range(n_groups):
            rows = slice(c * tg, (c + 1) * tg)
            x3 = x2_ref[rows, :] + acc_ref[c].T
            y_ref[rows, :] = _rms_norm(x3, gfin_ref[...]) if final_norm else x3

    @pl.when(jnp.logical_and(j == 0, s < last))
    def _():
        acc_ref[...] = jnp.zeros(acc_ref.shape, F32)
        for c in range(n_groups):
            for k in range(keys_per_step):
                route_weights(c, k)
            coef_even_ref[c] = gate_ref[c] * activations(c)


def _experts(xft, u_bf, vt_bf, e1, e2, th, x2, gfin, *, groups_per_tile, te, final_norm):
    t, d = x2.shape
    tg = xft.shape[2]
    gpt = groups_per_tile
    tm = gpt * tg
    n_exp = u_bf.shape[0]
    n_tiles = n_exp // te
    n_tok = t // tm
    n_steps = n_tok * n_tiles + 1

    def tok_now(s):
        return jnp.minimum(s // n_tiles, n_tok - 1)

    def tok_prev(s):
        return jnp.maximum(s - 1, 0) // n_tiles

    return pl.pallas_call(
        functools.partial(_experts_body, n_tiles=n_tiles, final_norm=final_norm),
        grid=(n_steps,),
        in_specs=[
            pl.BlockSpec((gpt, d, tg), lambda s: (tok_now(s), 0, 0)),
            pl.BlockSpec((te, d), lambda s: (s % n_tiles, 0)),
            pl.BlockSpec((d, te), lambda s: (0, jnp.maximum(s - 1, 0) % n_tiles)),
            pl.BlockSpec((gpt, PEER_HEADS, N_KEYS, tg), lambda s: (tok_now(s), 0, 0, 0)),
            pl.BlockSpec((gpt, PEER_HEADS, N_KEYS, tg), lambda s: (tok_now(s), 0, 0, 0)),
            pl.BlockSpec((gpt, PEER_HEADS, tg), lambda s: (tok_now(s), 0, 0)),
            pl.BlockSpec((tm, d), lambda s: (tok_prev(s), 0)),
            pl.BlockSpec((1, d), lambda s: (0, 0)),
        ],
        out_specs=pl.BlockSpec((tm, d), lambda s: (tok_prev(s), 0)),
        out_shape=jax.ShapeDtypeStruct((t, d), F32),
        scratch_shapes=[pltpu.VMEM((gpt, d, tg), F32), pltpu.VMEM((gpt, te, tg), BF16),
                        pltpu.VMEM((gpt, te, tg), BF16), pltpu.VMEM((gpt, te, tg), BF16)],
        compiler_params=pltpu.CompilerParams(
            dimension_semantics=("arbitrary",), vmem_limit_bytes=VMEM_LIMIT),
        name="peer_experts",
    )(xft, u_bf, vt_bf, e1, e2, th, x2, gfin.reshape(1, d))


def _pick_tile(n, pref):
    t = pref
    while n % t:
        t //= 2
    return t


def kernel(x_prompt, x_sample, state_conv, state_lru, g_mix, w_in, gmlp_ws, gmlp_bs, conv_w, conv_b,
           lru_wa, lru_ba, lru_wx, lru_bx, lru_lambda, w_out, g_ffn, peer_wq, peer_keys, peer_u,
           peer_v, g_final):
    batch, seq, d = x_prompt.shape
    dec_batch, dec_seq, _ = x_sample.shape
    depth = w_in.shape[0]
    assert batch == 1 and seq % GMLP_CHUNK == 0 and dec_seq <= CAUSAL_CHUNK
    assert d // LANES == gmlp_ws.shape[1] == lru_wa.shape[1]
    tp, ts = batch * seq, dec_batch * dec_seq

    idx = jnp.arange(GMLP_CHUNK)
    mask = (idx[:, None] // CAUSAL_CHUNK) >= (idx[None, :] // CAUSAL_CHUNK)
    hk = PEER_HEADS * N_KEYS

    xp = x_prompt.reshape(tp, d)
    xs = x_sample.reshape(ts, d)
    conv_p, lru_p, conv_s, lru_s, vrows_s = [], [], [], [], []
    for l in range(depth):
        last = l == depth - 1
        w_in_bf = w_in[l].astype(BF16)
        ws_p = (gmlp_ws[l] * mask.astype(F32)).astype(BF16)
        ws_s = ws_p[:, :dec_seq, :dec_seq]
        bias = jnp.repeat(gmlp_bs[l].T, LANES, axis=1)
        shared = (conv_w[l], conv_b[l].reshape(1, d), lru_wa[l].astype(BF16), lru_ba[l].reshape(1, d),
                  lru_wx[l].astype(BF16), lru_bx[l].reshape(1, d), lru_lambda[l].reshape(1, d),
                  w_out[l].astype(BF16))
        wqt = peer_wq[l].reshape(d, PEER_HEADS, 2, N_KEYS).transpose(2, 1, 3, 0).reshape(2 * hk, d)
        wqt = wqt.astype(BF16)
        keys_bf = peer_keys[l].astype(BF16)
        u_bf = peer_u[l].astype(BF16)
        vt_bf = peer_v[l].T.astype(BF16)
        gfin = g_final if last else jnp.ones((d,), F32)

        def ffn(x2):
            t = x2.shape[0]
            tm, tg = _pick_tile(t, 512), _pick_tile(t, 256)
            xft, e1, e2, th = _route(x2, g_ffn[l], wqt, keys_bf, tm=tg)
            return _experts(xft, u_bf, vt_bf, e1, e2, th, x2, gfin, groups_per_tile=tm // tg,
                            te=1024, final_norm=last)

        zp = _inproj(_prenorm(xp, g_mix[l], tm=_pick_tile(tp, 1024)), w_in_bf,
                     tm=_pick_tile(tp, 1024))
        zs = _inproj(_prenorm(xs, g_mix[l], tm=_pick_tile(ts, 1024)), w_in_bf,
                     tm=_pick_tile(ts, 1024))
        x2p, cp, lp = _mixer(zp, xp, 0, tp // GMLP_CHUNK, GMLP_CHUNK, ws_p, bias, *shared,
                             jnp.zeros((1, CONV_W - 1, d), F32), jnp.zeros((1, 1, d), F32),
                             carried=True)
        x2s, cs, ls = _mixer(zs, xs, 0, dec_batch, dec_seq, ws_s, bias[:dec_seq], *shared,
                             state_conv[l], state_lru[l].reshape(dec_batch, 1, d), carried=False)
        xp, xs = ffn(x2p), ffn(x2s)

        conv_p.append(cp)
        lru_p.append(lp.reshape(batch, d))
        conv_s.append(cs)
        lru_s.append(ls.reshape(dec_batch, d))
        vrows_s.append(zs[:, d:2 * d].reshape(dec_batch, dec_seq, d))

    return (xp.reshape(batch, seq, d), xs.reshape(dec_batch, dec_seq, d), jnp.stack(conv_p),
            jnp.stack(lru_p), jnp.stack(conv_s), jnp.stack(lru_s), jnp.stack(vrows_s))
```

```python
import functools
import math

import jax
import jax.numpy as jnp
from jax import lax
from jax.experimental import pallas as pl
from jax.experimental.pallas import tpu as pltpu

F32 = jnp.float32
BF16 = jnp.bfloat16

NORM_EPS = 1e-6
N_PARTS = 6
LANES = 128
SUBLANES = 8
CONV_W = 4
LRU_C = 8.0
GMLP_CHUNK = 128
CAUSAL_CHUNK = 64
N_KEYS = 128
PEER_HEADS = 8
PEER_TOPK = 16
VMEM_LIMIT = 56 * 1024 * 1024
ACC_SLAB = 256
INPROJ_SLAB = 256


def _gelu(x):
    c = math.sqrt(2.0 / math.pi)
    return x * (0.5 * (1.0 + jnp.tanh(c * (x + 0.044715 * (x * x * x)))))


def _gelu_twice(x):
    c = math.sqrt(2.0 / math.pi)
    return x * (1.0 + jnp.tanh(x * (c + (0.044715 * c) * (x * x))))


def _rms_norm(x, g):
    ms = jnp.mean(x * x, axis=-1, keepdims=True)
    return (x * lax.rsqrt(ms + NORM_EPS)) * g


def _prenorm_body(x_ref, g_ref, xn_ref):
    xn_ref[...] = _rms_norm(x_ref[...], g_ref[...]).astype(BF16)


def _prenorm(x, g, *, tm):
    t, d = x.shape
    return pl.pallas_call(
        _prenorm_body,
        grid=(t // tm,),
        in_specs=[pl.BlockSpec((tm, d), lambda i: (i, 0)), pl.BlockSpec((1, d), lambda i: (0, 0))],
        out_specs=pl.BlockSpec((tm, d), lambda i: (i, 0)),
        out_shape=jax.ShapeDtypeStruct((t, d), BF16),
        compiler_params=pltpu.CompilerParams(
            dimension_semantics=("arbitrary",), vmem_limit_bytes=VMEM_LIMIT),
        name="prenorm",
    )(x, g.reshape(1, d))


def _inproj_body(xn_ref, w_ref, z_ref):
    part = pl.program_id(0)

    def project(activation):
        for r in range(0, z_ref.shape[0], INPROJ_SLAB):
            rows = slice(r, r + INPROJ_SLAB)
            z_ref[rows, :] = activation(
                jnp.dot(xn_ref[rows, :], w_ref[...], preferred_element_type=F32))

    is_gelu = jnp.logical_or(part <= 1, part == 3)

    @pl.when(is_gelu)
    def _():
        project(_gelu)

    @pl.when(part == 2)
    def _():
        project(lambda z: z)

    @pl.when(part >= 4)
    def _():
        project(jax.nn.sigmoid)


def _inproj(xn, w_bf, *, tm):
    t, d = xn.shape
    n = w_bf.shape[1]
    tn = n // N_PARTS
    return pl.pallas_call(
        _inproj_body,
        grid=(N_PARTS, t // tm),
        in_specs=[
            pl.BlockSpec((tm, d), lambda p, i: (i, 0)),
            pl.BlockSpec((d, tn), lambda p, i: (0, p)),
        ],
        out_specs=pl.BlockSpec((tm, tn), lambda p, i: (i, p)),
        out_shape=jax.ShapeDtypeStruct((t, n), F32),
        compiler_params=pltpu.CompilerParams(
            dimension_semantics=("arbitrary", "arbitrary"), vmem_limit_bytes=VMEM_LIMIT),
        name="inproj",
    )(xn, w_bf)


def _scan_rows(a, b, h_in):
    sub = lax.broadcasted_iota(jnp.int32, (SUBLANES, a.shape[1]), 0)
    keep = {s: sub >= s for s in (1, 2, 4)}
    blocks = []
    carry = h_in
    for r in range(0, a.shape[0], SUBLANES):
        ab, bb = a[r:r + SUBLANES, :], b[r:r + SUBLANES, :]
        for s in (1, 2, 4):
            a_sh = jnp.where(keep[s], pltpu.roll(ab, s, axis=0), 1.0)
            b_sh = jnp.where(keep[s], pltpu.roll(bb, s, axis=0), 0.0)
            bb = ab * b_sh + bb
            ab = ab * a_sh
        hb = ab * carry + bb
        carry = hb[SUBLANES - 1:SUBLANES, :]
        blocks.append(hb)
    return jnp.concatenate(blocks, axis=0), carry


def _mixer_body(zu_ref, zv_ref, zx_ref, zg_ref, zga_ref, zgb_ref, x_ref,
                wg_ref, bg_ref, cw_ref, cb_ref, wa_ref, ba_ref, wx_ref, bx_ref, lam_ref,
                wout_ref, prev_ref, h0_ref,
                x2_ref, conv_ref, hlast_ref,
                xp_ref, merged_ref, hc_ref, *, carried):
    c = pl.program_id(0)
    rows_c = zx_ref.shape[0]
    pad = SUBLANES
    n_groups = zx_ref.shape[1] // LANES

    if carried:
        @pl.when(c == 0)
        def _():
            xp_ref[0:pad, :] = jnp.zeros((pad, xp_ref.shape[1]), F32)
            hc_ref[...] = jnp.zeros(hc_ref.shape, F32)
    else:
        xp_ref[pad - (CONV_W - 1):pad, :] = prev_ref[0]
        hc_ref[0:1, :] = h0_ref[0]
    xp_ref[pad:pad + rows_c, :] = zx_ref[...]

    def group(j, carry):
        col = pl.ds(pl.multiple_of(j * LANES, LANES), LANES)
        s = jnp.dot(wg_ref[j], zv_ref[:, col].astype(BF16), preferred_element_type=F32) + bg_ref[:, col]
        y_a = zu_ref[:, col] * s
        xc = cb_ref[:, col]
        for k in range(CONV_W):
            lo = pad - (CONV_W - 1) + k
            xc = xc + xp_ref[lo:lo + rows_c, col] * cw_ref[k:k + 1, col]
        xcb = xc.astype(BF16)
        r = jax.nn.sigmoid(jnp.dot(xcb, wa_ref[j], preferred_element_type=F32) + ba_ref[:, col])
        i = jax.nn.sigmoid(jnp.dot(xcb, wx_ref[j], preferred_element_type=F32) + bx_ref[:, col])
        nlam = -lam_ref[:, col]
        softplus = jnp.maximum(nlam, 0.0) + jnp.log1p(jnp.exp(-jnp.abs(nlam)))
        log_a = (-LRU_C) * r * softplus
        a = jnp.exp(log_a)
        mult = jnp.sqrt(jnp.tanh(-log_a) * (a * a + 1.0))
        if carried:
            rows = lax.broadcasted_iota(jnp.int32, mult.shape, 0)
            mult = jnp.where(jnp.logical_and(rows == 0, c == 0), 1.0, mult)
        b = mult * i * xc
        h, h_last = _scan_rows(a, b, hc_ref[0:1, col])
        hc_ref[0:1, col] = h_last
        y_b = h * zg_ref[:, col]
        merged_ref[:, col] = (zga_ref[:, col] * y_a + zgb_ref[:, col] * y_b).astype(BF16)
        return carry

    lax.fori_loop(0, n_groups, group, 0, unroll=4)

    x2_ref[...] = x_ref[...] + jnp.dot(merged_ref[...], wout_ref[...], preferred_element_type=F32)
    conv_ref[0] = xp_ref[pad + rows_c - (CONV_W - 1):pad + rows_c, :]
    hlast_ref[0] = hc_ref[0:1, :]
    if carried:
        xp_ref[0:pad, :] = xp_ref[rows_c:rows_c + pad, :]


def _mixer(z, x, row0, n_seq, rows_c, wg, bg, cw, cb, wa, ba, wx, bx, lam, wout, prev, h0, *, carried):
    d = x.shape[1]
    blk0 = row0 // rows_c
    n_state = prev.shape[0]

    def zspec(part):
        return pl.BlockSpec((rows_c, d), lambda c, part=part: (blk0 + c, part))

    def const(shape):
        return pl.BlockSpec(shape, lambda c: (0,) * len(shape))

    def state_idx(c):
        return (c, 0, 0) if not carried else (0, 0, 0)

    return pl.pallas_call(
        functools.partial(_mixer_body, carried=carried),
        grid=(n_seq,),
        in_specs=[zspec(p) for p in range(N_PARTS)] + [
            pl.BlockSpec((rows_c, d), lambda c: (blk0 + c, 0)),
            const(wg.shape), const(bg.shape), const(cw.shape), const(cb.shape),
            const(wa.shape), const(ba.shape), const(wx.shape), const(bx.shape), const(lam.shape),
            const(wout.shape),
            pl.BlockSpec((1, CONV_W - 1, d), state_idx),
            pl.BlockSpec((1, 1, d), state_idx),
        ],
        out_specs=[
            pl.BlockSpec((rows_c, d), lambda c: (c, 0)),
            pl.BlockSpec((1, CONV_W - 1, d), state_idx),
            pl.BlockSpec((1, 1, d), state_idx),
        ],
        out_shape=[
            jax.ShapeDtypeStruct((n_seq * rows_c, d), F32),
            jax.ShapeDtypeStruct((n_state, CONV_W - 1, d), F32),
            jax.ShapeDtypeStruct((n_state, 1, d), F32),
        ],
        scratch_shapes=[
            pltpu.VMEM((rows_c + 2 * SUBLANES, d), F32),
            pltpu.VMEM((rows_c, d), BF16),
            pltpu.VMEM((SUBLANES, d), F32),
        ],
        compiler_params=pltpu.CompilerParams(
            dimension_semantics=("arbitrary",), vmem_limit_bytes=VMEM_LIMIT),
        name="mixer_stream" if carried else "mixer_step",
    )(z, z, z, z, z, z, x, wg, bg, cw, cb, wa, ba, wx, bx, lam, wout, prev, h0)


def _sort_network(n):
    pairs = []
    p = 1
    while p < n:
        k = p
        while k >= 1:
            for j in range(k % p, n - k, 2 * k):
                for i in range(min(k, n - j - k)):
                    if (i + j) // (2 * p) == (i + j + k) // (2 * p):
                        pairs.append((i + j, i + j + k))
            k //= 2
        p *= 2
    return pairs


_SORT16 = _sort_network(PEER_TOPK)


def _sort_desc(v):
    v = list(v)
    for i, j in _SORT16:
        v[i], v[j] = jnp.maximum(v[i], v[j]), jnp.minimum(v[i], v[j])
    return v


def _merge_top(xs, ys):
    z = [jnp.maximum(xs[i], ys[PEER_TOPK - 1 - i]) for i in range(PEER_TOPK)]
    d = PEER_TOPK // 2
    while d >= 1:
        for i in range(PEER_TOPK):
            if (i & d) == 0:
                z[i], z[i + d] = jnp.maximum(z[i], z[i + d]), jnp.minimum(z[i], z[i + d])
        d //= 2
    return z


def _top_of_lists(lists):
    while len(lists) > 1:
        lists = [_merge_top(lists[i], lists[i + 1]) for i in range(0, len(lists), 2)]
    return lists[0]


def _round_bf16(x):
    return x.astype(BF16).astype(F32)


def _route_body(x2_ref, g_ref, wqt_ref, keys_ref, xft_ref, e1_ref, e2_ref, th_ref, s_ref):
    tm = x2_ref.shape[0]
    hk = PEER_HEADS * N_KEYS
    xft = _rms_norm(x2_ref[...], g_ref[...]).T.astype(BF16)
    xft_ref[0] = xft
    qt = jnp.dot(wqt_ref[...], xft, preferred_element_type=F32)
    for p in range(2):
        for h in range(PEER_HEADS):
            lo = p * hk + h * N_KEYS
            q_hp = qt[lo:lo + N_KEYS, :].astype(BF16)
            s_hp = jnp.dot(keys_ref[p], q_hp, preferred_element_type=F32)
            for g in range(tm // LANES):
                s_ref[p, g, h * N_KEYS:(h + 1) * N_KEYS, :] = s_hp[:, g * LANES:(g + 1) * LANES]

    def lane_group(lg, carry):
        lanes = pl.ds(pl.multiple_of(lg * LANES, LANES), LANES)
        tops = []
        for p in range(2):
            vals = [s_ref[p, lg, pl.ds(n, PEER_HEADS, stride=N_KEYS), :] for n in range(N_KEYS)]
            groups = [_sort_desc(vals[i:i + PEER_TOPK]) for i in range(0, N_KEYS, PEER_TOPK)]
            tops.append(_top_of_lists(groups))
        m1, m2 = tops[0][0], tops[1][0]
        e1 = [jnp.exp(v - m1) for v in tops[0]]
        e2 = [jnp.exp(v - m2) for v in tops[1]]
        best = _top_of_lists([[e1[i] * e2[j] for j in range(PEER_TOPK)] for i in range(PEER_TOPK)])
        z = best[0]
        for v in best[1:]:
            z = z + v
        rz = 0.5 / z
        e1n = [_round_bf16(v * rz) for v in e1]
        e2n = [_round_bf16(v) for v in e2]
        bestn = _top_of_lists([[_round_bf16(e1n[i] * e2n[j]) for j in range(PEER_TOPK)]
                               for i in range(PEER_TOPK)])
        th_ref[0, :, lanes] = bestn[PEER_TOPK - 1]
        for h in range(PEER_HEADS):
            rows = slice(h * N_KEYS, (h + 1) * N_KEYS)
            e1_ref[0, h, :, lanes] = jnp.exp(s_ref[0, lg, rows, :] - m1[h:h + 1, :]) * rz[h:h + 1, :]
            e2_ref[0, h, :, lanes] = jnp.exp(s_ref[1, lg, rows, :] - m2[h:h + 1, :]).astype(BF16)
        return carry

    lax.fori_loop(0, tm // LANES, lane_group, 0)


def _route(x2, g, wqt, keys_bf, *, tm):
    t, d = x2.shape
    hk = PEER_HEADS * N_KEYS
    n_groups = t // tm
    return pl.pallas_call(
        _route_body,
        grid=(t // tm,),
        in_specs=[
            pl.BlockSpec((tm, d), lambda i: (i, 0)),
            pl.BlockSpec((1, d), lambda i: (0, 0)),
            pl.BlockSpec(wqt.shape, lambda i: (0, 0)),
            pl.BlockSpec(keys_bf.shape, lambda i: (0, 0, 0)),
        ],
        out_specs=[
            pl.BlockSpec((1, d, tm), lambda i: (i, 0, 0)),
            pl.BlockSpec((1, PEER_HEADS, N_KEYS, tm), lambda i: (i, 0, 0, 0)),
            pl.BlockSpec((1, PEER_HEADS, N_KEYS, tm), lambda i: (i, 0, 0, 0)),
            pl.BlockSpec((1, PEER_HEADS, tm), lambda i: (i, 0, 0)),
        ],
        out_shape=[
            jax.ShapeDtypeStruct((n_groups, d, tm), BF16),
            jax.ShapeDtypeStruct((n_groups, PEER_HEADS, N_KEYS, tm), F32),
            jax.ShapeDtypeStruct((n_groups, PEER_HEADS, N_KEYS, tm), BF16),
            jax.ShapeDtypeStruct((n_groups, PEER_HEADS, tm), F32),
        ],
        scratch_shapes=[pltpu.VMEM((2, tm // LANES, hk, LANES), F32)],
        compiler_params=pltpu.CompilerParams(
            dimension_semantics=("arbitrary",), vmem_limit_bytes=VMEM_LIMIT),
        name="peer_route",
    )(x2, g.reshape(1, d), wqt, keys_bf)


def _rows_bf16(row):
    tile = jnp.broadcast_to(row, (2 * SUBLANES, row.shape[1])).astype(BF16)
    return jnp.tile(tile, (N_KEYS // (2 * SUBLANES), 1))


def _fold_rows_min(x, rows):
    while x.shape[0] > rows:
        half = x.shape[0] // 2
        x = jnp.minimum(x[:half], x[half:])
    return x


def _experts_body(xft_ref, u_ref, vt_ref, e1_ref, e2_ref, th_ref, x2_ref, gfin_ref,
                  y_ref, acc_ref, gate_ref, coef_even_ref, coef_odd_ref, *, n_tiles, final_norm):
    s = pl.program_id(0)
    last = pl.num_programs(0) - 1
    n_groups, d, tg = xft_ref.shape
    te = u_ref.shape[0]
    keys_per_step = te // N_KEYS
    j = s % n_tiles
    key0 = j * keys_per_step
    slab_rows_per_key = d // keys_per_step
    assert n_tiles % 2 == 0

    def project_values(c, prev_ref, rows):
        step = min(ACC_SLAB, rows.stop - rows.start)
        for r in range(rows.start, rows.stop, step):
            acc_ref[c, r:r + step, :] += jnp.dot(vt_ref[r:r + step, :], prev_ref[c],
                                                 preferred_element_type=F32)

    def route_weights(c, k):
        rows = slice(k * N_KEYS, (k + 1) * N_KEYS)
        e1_rows = [e1_ref[c, h, pl.ds(key0 + k, 1), :] for h in range(PEER_HEADS)]
        low = None
        for lg in range(tg // LANES):
            lanes = slice(lg * LANES, (lg + 1) * LANES)
            gate = None
            for h in range(PEER_HEADS):
                prod = _rows_bf16(e1_rows[h][:, lanes]) * e2_ref[c, h, :, lanes]
                sel = jnp.where(prod >= _rows_bf16(th_ref[c, h:h + 1, lanes]), prod, 0.0)
                gate = sel if gate is None else gate + sel
            gate_ref[c, rows, lanes] = gate
            low = gate if low is None else jnp.minimum(low, gate)
        return _fold_rows_min(low, 2 * SUBLANES)

    def activations(c):
        x = jnp.dot(u_ref[...], xft_ref[c], preferred_element_type=F32)
        return _gelu_twice(x).astype(BF16)

    def steady(prev_ref, next_ref):
        keys_per_slab = max(1, ACC_SLAB // slab_rows_per_key)
        for c in range(n_groups):
            act = activations(c)
            for k0 in range(0, keys_per_step, keys_per_slab):
                low = route_weights(c, k0)
                for k in range(k0 + 1, k0 + keys_per_slab):
                    low = jnp.minimum(low, route_weights(c, k))
                if c or k0:
                    prev_ref[c, 0:2 * SUBLANES, 0:LANES] += jnp.minimum(low, 0.0)
                project_values(c, prev_ref, slice(k0 * slab_rows_per_key,
                                                  (k0 + keys_per_slab) * slab_rows_per_key))
            next_ref[c] = gate_ref[c] * act

    @pl.when(jnp.logical_and(j != 0, j % 2 == 0))
    def _():
        steady(coef_odd_ref, coef_even_ref)

    @pl.when(j % 2 == 1)
    def _():
        steady(coef_even_ref, coef_odd_ref)

    @pl.when(jnp.logical_and(j == 0, s > 0))
    def _():
        for c in range(n_groups):
            project_values(c, coef_odd_ref, slice(0, d))
        for c in range(n_groups):
            rows = slice(c * tg, (c + 1) * tg)
            x3 = x2_ref[rows, :] + acc_ref[c].T
            y_ref[rows, :] = _rms_norm(x3, gfin_ref[...]) if final_norm else x3

    @pl.when(jnp.logical_and(j == 0, s < last))
    def _():
        acc_ref[...] = jnp.zeros(acc_ref.shape, F32)
        for c in range(n_groups):
            for k in range(keys_per_step):
                route_weights(c, k)
            coef_even_ref[c] = gate_ref[c] * activations(c)


def _experts(xft, u_bf, vt_bf, e1, e2, th, x2, gfin, *, groups_per_tile, te, final_norm):
    t, d = x2.shape
    tg = xft.shape[2]
    gpt = groups_per_tile
    tm = gpt * tg
    n_exp = u_bf.shape[0]
    n_tiles = n_exp // te
    n_tok = t // tm
    n_steps = n_tok * n_tiles + 1

    def tok_now(s):
        return jnp.minimum(s // n_tiles, n_tok - 1)

    def tok_prev(s):
        return jnp.maximum(s - 1, 0) // n_tiles

    return pl.pallas_call(
        functools.partial(_experts_body, n_tiles=n_tiles, final_norm=final_norm),
        grid=(n_steps,),
        in_specs=[
            pl.BlockSpec((gpt, d, tg), lambda s: (tok_now(s), 0, 0)),
            pl.BlockSpec((te, d), lambda s: (s % n_tiles, 0)),
            pl.BlockSpec((d, te), lambda s: (0, jnp.maximum(s - 1, 0) % n_tiles)),
            pl.BlockSpec((gpt, PEER_HEADS, N_KEYS, tg), lambda s: (tok_now(s), 0, 0, 0)),
            pl.BlockSpec((gpt, PEER_HEADS, N_KEYS, tg), lambda s: (tok_now(s), 0, 0, 0)),
            pl.BlockSpec((gpt, PEER_HEADS, tg), lambda s: (tok_now(s), 0, 0)),
            pl.BlockSpec((tm, d), lambda s: (tok_prev(s), 0)),
            pl.BlockSpec((1, d), lambda s: (0, 0)),
        ],
        out_specs=pl.BlockSpec((tm, d), lambda s: (tok_prev(s), 0)),
        out_shape=jax.ShapeDtypeStruct((t, d), F32),
        scratch_shapes=[pltpu.VMEM((gpt, d, tg), F32), pltpu.VMEM((gpt, te, tg), BF16),
                        pltpu.VMEM((gpt, te, tg), BF16), pltpu.VMEM((gpt, te, tg), BF16)],
        compiler_params=pltpu.CompilerParams(
            dimension_semantics=("arbitrary",), vmem_limit_bytes=VMEM_LIMIT),
        name="peer_experts",
    )(xft, u_bf, vt_bf, e1, e2, th, x2, gfin.reshape(1, d))


def _pick_tile(n, pref):
    t = pref
    while n % t:
        t //= 2
    return t


def kernel(x_prompt, x_sample, state_conv, state_lru, g_mix, w_in, gmlp_ws, gmlp_bs, conv_w, conv_b,
           lru_wa, lru_ba, lru_wx, lru_bx, lru_lambda, w_out, g_ffn, peer_wq, peer_keys, peer_u,
           peer_v, g_final):
    batch, seq, d = x_prompt.shape
    dec_batch, dec_seq, _ = x_sample.shape
    depth = w_in.shape[0]
    assert batch == 1 and seq % GMLP_CHUNK == 0 and dec_seq <= CAUSAL_CHUNK
    assert d // LANES == gmlp_ws.shape[1] == lru_wa.shape[1]
    tp, ts = batch * seq, dec_batch * dec_seq

    idx = jnp.arange(GMLP_CHUNK)
    mask = (idx[:, None] // CAUSAL_CHUNK) >= (idx[None, :] // CAUSAL_CHUNK)
    hk = PEER_HEADS * N_KEYS

    xp = x_prompt.reshape(tp, d)
    xs = x_sample.reshape(ts, d)
    conv_p, lru_p, conv_s, lru_s, vrows_s = [], [], [], [], []
    for l in range(depth):
        last = l == depth - 1
        w_in_bf = w_in[l].astype(BF16)
        ws_p = (gmlp_ws[l] * mask.astype(F32)).astype(BF16)
        ws_s = ws_p[:, :dec_seq, :dec_seq]
        bias = jnp.repeat(gmlp_bs[l].T, LANES, axis=1)
        shared = (conv_w[l], conv_b[l].reshape(1, d), lru_wa[l].astype(BF16), lru_ba[l].reshape(1, d),
                  lru_wx[l].astype(BF16), lru_bx[l].reshape(1, d), lru_lambda[l].reshape(1, d),
                  w_out[l].astype(BF16))
        wqt = peer_wq[l].reshape(d, PEER_HEADS, 2, N_KEYS).transpose(2, 1, 3, 0).reshape(2 * hk, d)
        wqt = wqt.astype(BF16)
        keys_bf = peer_keys[l].astype(BF16)
        u_bf = peer_u[l].astype(BF16)
        vt_bf = peer_v[l].T.astype(BF16)
        gfin = g_final if last else jnp.ones((d,), F32)

        def ffn(x2):
            t = x2.shape[0]
            tm, tg = _pick_tile(t, 512), _pick_tile(t, 256)
            xft, e1, e2, th = _route(x2, g_ffn[l], wqt, keys_bf, tm=tg)
            return _experts(xft, u_bf, vt_bf, e1, e2, th, x2, gfin, groups_per_tile=tm // tg,
                            te=1024, final_norm=last)

        zp = _inproj(_prenorm(xp, g_mix[l], tm=_pick_tile(tp, 1024)), w_in_bf,
                     tm=_pick_tile(tp, 1024))
        zs = _inproj(_prenorm(xs, g_mix[l], tm=_pick_tile(ts, 1024)), w_in_bf,
                     tm=_pick_tile(ts, 1024))
        x2p, cp, lp = _mixer(zp, xp, 0, tp // GMLP_CHUNK, GMLP_CHUNK, ws_p, bias, *shared,
                             jnp.zeros((1, CONV_W - 1, d), F32), jnp.zeros((1, 1, d), F32),
                             carried=True)
        x2s, cs, ls = _mixer(zs, xs, 0, dec_batch, dec_seq, ws_s, bias[:dec_seq], *shared,
                             state_conv[l], state_lru[l].reshape(dec_batch, 1, d), carried=False)
        xp, xs = ffn(x2p), ffn(x2s)

        conv_p.append(cp)
        lru_p.append(lp.reshape(batch, d))
        conv_s.append(cs)
        lru_s.append(ls.reshape(dec_batch, d))
        vrows_s.append(zs[:, d:2 * d].reshape(dec_batch, dec_seq, d))

    return (xp.reshape(batch, seq, d), xs.reshape(dec_batch, dec_seq, d), jnp.stack(conv_p),
            jnp.stack(lru_p), jnp.stack(conv_s), jnp.stack(lru_s), jnp.stack(vrows_s))
```

```python
import functools
import math

import jax
import jax.numpy as jnp
from jax import lax
from jax.experimental import pallas as pl
from jax.experimental.pallas import tpu as pltpu

F32 = jnp.float32
BF16 = jnp.bfloat16

NORM_EPS = 1e-6
N_PARTS = 6
LANES = 128
SUBLANES = 8
CONV_W = 4
LRU_C = 8.0
GMLP_CHUNK = 128
CAUSAL_CHUNK = 64
N_KEYS = 128
PEER_HEADS = 8
PEER_TOPK = 16
VMEM_LIMIT = 56 * 1024 * 1024
ACC_SLAB = 256
INPROJ_SLAB = 256


def _gelu(x):
    c = math.sqrt(2.0 / math.pi)
    return x * (0.5 * (1.0 + jnp.tanh(c * (x + 0.044715 * (x * x * x)))))


def _gelu_twice(x):
    c = math.sqrt(2.0 / math.pi)
    return x * (1.0 + jnp.tanh(x * (c + (0.044715 * c) * (x * x))))


def _rms_norm(x, g):
    ms = jnp.mean(x * x, axis=-1, keepdims=True)
    return (x * lax.rsqrt(ms + NORM_EPS)) * g


def _prenorm_body(x_ref, g_ref, xn_ref):
    xn_ref[...] = _rms_norm(x_ref[...], g_ref[...]).astype(BF16)


def _prenorm(x, g, *, tm):
    t, d = x.shape
    return pl.pallas_call(
        _prenorm_body,
        grid=(t // tm,),
        in_specs=[pl.BlockSpec((tm, d), lambda i: (i, 0)), pl.BlockSpec((1, d), lambda i: (0, 0))],
        out_specs=pl.BlockSpec((tm, d), lambda i: (i, 0)),
        out_shape=jax.ShapeDtypeStruct((t, d), BF16),
        compiler_params=pltpu.CompilerParams(
            dimension_semantics=("arbitrary",), vmem_limit_bytes=VMEM_LIMIT),
        name="prenorm",
    )(x, g.reshape(1, d))


def _inproj_body(xn_ref, w_ref, z_ref):
    part = pl.program_id(0)

    def project(activation):
        for r in range(0, z_ref.shape[0], INPROJ_SLAB):
            rows = slice(r, r + INPROJ_SLAB)
            z_ref[rows, :] = activation(
                jnp.dot(xn_ref[rows, :], w_ref[...], preferred_element_type=F32))

    is_gelu = jnp.logical_or(part <= 1, part == 3)

    @pl.when(is_gelu)
    def _():
        project(_gelu)

    @pl.when(part == 2)
    def _():
        project(lambda z: z)

    @pl.when(part >= 4)
    def _():
        project(jax.nn.sigmoid)


def _inproj(xn, w_bf, *, tm):
    t, d = xn.shape
    n = w_bf.shape[1]
    tn = n // N_PARTS
    return pl.pallas_call(
        _inproj_body,
        grid=(N_PARTS, t // tm),
        in_specs=[
            pl.BlockSpec((tm, d), lambda p, i: (i, 0)),
            pl.BlockSpec((d, tn), lambda p, i: (0, p)),
        ],
        out_specs=pl.BlockSpec((tm, tn), lambda p, i: (i, p)),
        out_shape=jax.ShapeDtypeStruct((t, n), F32),
        compiler_params=pltpu.CompilerParams(
            dimension_semantics=("arbitrary", "arbitrary"), vmem_limit_bytes=VMEM_LIMIT),
        name="inproj",
    )(xn, w_bf)


def _scan_rows(a, b, h_in):
    sub = lax.broadcasted_iota(jnp.int32, (SUBLANES, a.shape[1]), 0)
    keep = {s: sub >= s for s in (1, 2, 4)}
    blocks = []
    carry = h_in
    for r in range(0, a.shape[0], SUBLANES):
        ab, bb = a[r:r + SUBLANES, :], b[r:r + SUBLANES, :]
        for s in (1, 2, 4):
            a_sh = jnp.where(keep[s], pltpu.roll(ab, s, axis=0), 1.0)
            b_sh = jnp.where(keep[s], pltpu.roll(bb, s, axis=0), 0.0)
            bb = ab * b_sh + bb
            ab = ab * a_sh
        hb = ab * carry + bb
        carry = hb[SUBLANES - 1:SUBLANES, :]
        blocks.append(hb)
    return jnp.concatenate(blocks, axis=0), carry


def _mixer_body(zu_ref, zv_ref, zx_ref, zg_ref, zga_ref, zgb_ref, x_ref,
                wg_ref, bg_ref, cw_ref, cb_ref, wa_ref, ba_ref, wx_ref, bx_ref, lam_ref,
                wout_ref, prev_ref, h0_ref,
                x2_ref, conv_ref, hlast_ref,
                xp_ref, merged_ref, hc_ref, *, carried):
    c = pl.program_id(0)
    rows_c = zx_ref.shape[0]
    pad = SUBLANES
    n_groups = zx_ref.shape[1] // LANES

    if carried:
        @pl.when(c == 0)
        def _():
            xp_ref[0:pad, :] = jnp.zeros((pad, xp_ref.shape[1]), F32)
            hc_ref[...] = jnp.zeros(hc_ref.shape, F32)
    else:
        xp_ref[pad - (CONV_W - 1):pad, :] = prev_ref[0]
        hc_ref[0:1, :] = h0_ref[0]
    xp_ref[pad:pad + rows_c, :] = zx_ref[...]

    def group(j, carry):
        col = pl.ds(pl.multiple_of(j * LANES, LANES), LANES)
        s = jnp.dot(wg_ref[j], zv_ref[:, col].astype(BF16), preferred_element_type=F32) + bg_ref[:, col]
        y_a = zu_ref[:, col] * s
        xc = cb_ref[:, col]
        for k in range(CONV_W):
            lo = pad - (CONV_W - 1) + k
            xc = xc + xp_ref[lo:lo + rows_c, col] * cw_ref[k:k + 1, col]
        xcb = xc.astype(BF16)
        r = jax.nn.sigmoid(jnp.dot(xcb, wa_ref[j], preferred_element_type=F32) + ba_ref[:, col])
        i = jax.nn.sigmoid(jnp.dot(xcb, wx_ref[j], preferred_element_type=F32) + bx_ref[:, col])
        nlam = -lam_ref[:, col]
        softplus = jnp.maximum(nlam, 0.0) + jnp.log1p(jnp.exp(-jnp.abs(nlam)))
        log_a = (-LRU_C) * r * softplus
        a = jnp.exp(log_a)
        mult = jnp.sqrt(jnp.tanh(-log_a) * (a * a + 1.0))
        if carried:
            rows = lax.broadcasted_iota(jnp.int32, mult.shape, 0)
            mult = jnp.where(jnp.logical_and(rows == 0, c == 0), 1.0, mult)
        b = mult * i * xc
        h, h_last = _scan_rows(a, b, hc_ref[0:1, col])
        hc_ref[0:1, col] = h_last
        y_b = h * zg_ref[:, col]
        merged_ref[:, col] = (zga_ref[:, col] * y_a + zgb_ref[:, col] * y_b).astype(BF16)
        return carry

    lax.fori_loop(0, n_groups, group, 0, unroll=4)

    x2_ref[...] = x_ref[...] + jnp.dot(merged_ref[...], wout_ref[...], preferred_element_type=F32)
    conv_ref[0] = xp_ref[pad + rows_c - (CONV_W - 1):pad + rows_c, :]
    hlast_ref[0] = hc_ref[0:1, :]
    if carried:
        xp_ref[0:pad, :] = xp_ref[rows_c:rows_c + pad, :]


def _mixer(z, x, row0, n_seq, rows_c, wg, bg, cw, cb, wa, ba, wx, bx, lam, wout, prev, h0, *, carried):
    d = x.shape[1]
    blk0 = row0 // rows_c
    n_state = prev.shape[0]

    def zspec(part):
        return pl.BlockSpec((rows_c, d), lambda c, part=part: (blk0 + c, part))

    def const(shape):
        return pl.BlockSpec(shape, lambda c: (0,) * len(shape))

    def state_idx(c):
        return (c, 0, 0) if not carried else (0, 0, 0)

    return pl.pallas_call(
        functools.partial(_mixer_body, carried=carried),
        grid=(n_seq,),
        in_specs=[zspec(p) for p in range(N_PARTS)] + [
            pl.BlockSpec((rows_c, d), lambda c: (blk0 + c, 0)),
            const(wg.shape), const(bg.shape), const(cw.shape), const(cb.shape),
            const(wa.shape), const(ba.shape), const(wx.shape), const(bx.shape), const(lam.shape),
            const(wout.shape),
            pl.BlockSpec((1, CONV_W - 1, d), state_idx),
            pl.BlockSpec((1, 1, d), state_idx),
        ],
        out_specs=[
            pl.BlockSpec((rows_c, d), lambda c: (c, 0)),
            pl.BlockSpec((1, CONV_W - 1, d), state_idx),
            pl.BlockSpec((1, 1, d), state_idx),
        ],
        out_shape=[
            jax.ShapeDtypeStruct((n_seq * rows_c, d), F32),
            jax.ShapeDtypeStruct((n_state, CONV_W - 1, d), F32),
            jax.ShapeDtypeStruct((n_state, 1, d), F32),
        ],
        scratch_shapes=[
            pltpu.VMEM((rows_c + 2 * SUBLANES, d), F32),
            pltpu.VMEM((rows_c, d), BF16),
            pltpu.VMEM((SUBLANES, d), F32),
        ],
        compiler_params=pltpu.CompilerParams(
            dimension_semantics=("arbitrary",), vmem_limit_bytes=VMEM_LIMIT),
        name="mixer_stream" if carried else "mixer_step",
    )(z, z, z, z, z, z, x, wg, bg, cw, cb, wa, ba, wx, bx, lam, wout, prev, h0)


def _sort_network(n):
    pairs = []
    p = 1
    while p < n:
        k = p
        while k >= 1:
            for j in range(k % p, n - k, 2 * k):
                for i in range(min(k, n - j - k)):
                    if (i + j) // (2 * p) == (i + j + k) // (2 * p):
                        pairs.append((i + j, i + j + k))
            k //= 2
        p *= 2
    return pairs


_SORT16 = _sort_network(PEER_TOPK)


def _sort_desc(v):
    v = list(v)
    for i, j in _SORT16:
        v[i], v[j] = jnp.maximum(v[i], v[j]), jnp.minimum(v[i], v[j])
    return v


def _merge_top(xs, ys):
    z = [jnp.maximum(xs[i], ys[PEER_TOPK - 1 - i]) for i in range(PEER_TOPK)]
    d = PEER_TOPK // 2
    while d >= 1:
        for i in range(PEER_TOPK):
            if (i & d) == 0:
                z[i], z[i + d] = jnp.maximum(z[i], z[i + d]), jnp.minimum(z[i], z[i + d])
        d //= 2
    return z


def _top_of_lists(lists):
    while len(lists) > 1:
        lists = [_merge_top(lists[i], lists[i + 1]) for i in range(0, len(lists), 2)]
    return lists[0]


def _round_bf16(x):
    return x.astype(BF16).astype(F32)


def _route_body(x2_ref, g_ref, wqt_ref, keys_ref, xft_ref, e1_ref, e2_ref, th_ref, s_ref):
    tm = x2_ref.shape[0]
    hk = PEER_HEADS * N_KEYS
    xft = _rms_norm(x2_ref[...], g_ref[...]).T.astype(BF16)
    xft_ref[0] = xft
    qt = jnp.dot(wqt_ref[...], xft, preferred_element_type=F32)
    for p in range(2):
        for h in range(PEER_HEADS):
            lo = p * hk + h * N_KEYS
            q_hp = qt[lo:lo + N_KEYS, :].astype(BF16)
            s_hp = jnp.dot(keys_ref[p], q_hp, preferred_element_type=F32)
            for g in range(tm // LANES):
                s_ref[p, g, h * N_KEYS:(h + 1) * N_KEYS, :] = s_hp[:, g * LANES:(g + 1) * LANES]

    def lane_group(lg, carry):
        lanes = pl.ds(pl.multiple_of(lg * LANES, LANES), LANES)
        tops = []
        for p in range(2):
            vals = [s_ref[p, lg, pl.ds(n, PEER_HEADS, stride=N_KEYS), :] for n in range(N_KEYS)]
            groups = [_sort_desc(vals[i:i + PEER_TOPK]) for i in range(0, N_KEYS, PEER_TOPK)]
            tops.append(_top_of_lists(groups))
        m1, m2 = tops[0][0], tops[1][0]
        e1 = [jnp.exp(v - m1) for v in tops[0]]
        e2 = [jnp.exp(v - m2) for v in tops[1]]
        best = _top_of_lists([[e1[i] * e2[j] for j in range(PEER_TOPK)] for i in range(PEER_TOPK)])
        z = best[0]
        for v in best[1:]:
            z = z + v
        rz = 0.5 / z
        e1n = [_round_bf16(v * rz) for v in e1]
        e2n = [_round_bf16(v) for v in e2]
        bestn = _top_of_lists([[_round_bf16(e1n[i] * e2n[j]) for j in range(PEER_TOPK)]
                               for i in range(PEER_TOPK)])
        th_ref[0, :, lanes] = bestn[PEER_TOPK - 1]
        for h in range(PEER_HEADS):
            rows = slice(h * N_KEYS, (h + 1) * N_KEYS)
            e1_ref[0, h, :, lanes] = jnp.exp(s_ref[0, lg, rows, :] - m1[h:h + 1, :]) * rz[h:h + 1, :]
            e2_ref[0, h, :, lanes] = jnp.exp(s_ref[1, lg, rows, :] - m2[h:h + 1, :]).astype(BF16)
        return carry

    lax.fori_loop(0, tm // LANES, lane_group, 0)


def _route(x2, g, wqt, keys_bf, *, tm):
    t, d = x2.shape
    hk = PEER_HEADS * N_KEYS
    n_groups = t // tm
    return pl.pallas_call(
        _route_body,
        grid=(t // tm,),
        in_specs=[
            pl.BlockSpec((tm, d), lambda i: (i, 0)),
            pl.BlockSpec((1, d), lambda i: (0, 0)),
            pl.BlockSpec(wqt.shape, lambda i: (0, 0)),
            pl.BlockSpec(keys_bf.shape, lambda i: (0, 0, 0)),
        ],
        out_specs=[
            pl.BlockSpec((1, d, tm), lambda i: (i, 0, 0)),
            pl.BlockSpec((1, PEER_HEADS, N_KEYS, tm), lambda i: (i, 0, 0, 0)),
            pl.BlockSpec((1, PEER_HEADS, N_KEYS, tm), lambda i: (i, 0, 0, 0)),
            pl.BlockSpec((1, PEER_HEADS, tm), lambda i: (i, 0, 0)),
        ],
        out_shape=[
            jax.ShapeDtypeStruct((n_groups, d, tm), BF16),
            jax.ShapeDtypeStruct((n_groups, PEER_HEADS, N_KEYS, tm), F32),
            jax.ShapeDtypeStruct((n_groups, PEER_HEADS, N_KEYS, tm), BF16),
            jax.ShapeDtypeStruct((n_groups, PEER_HEADS, tm), F32),
        ],
        scratch_shapes=[pltpu.VMEM((2, tm // LANES, hk, LANES), F32)],
        compiler_params=pltpu.CompilerParams(
            dimension_semantics=("arbitrary",), vmem_limit_bytes=VMEM_LIMIT),
        name="peer_route",
    )(x2, g.reshape(1, d), wqt, keys_bf)


def _rows_bf16(row):
    tile = jnp.broadcast_to(row, (2 * SUBLANES, row.shape[1])).astype(BF16)
    return jnp.tile(tile, (N_KEYS // (2 * SUBLANES), 1))


def _fold_rows_min(x, rows):
    while x.shape[0] > rows:
        half = x.shape[0] // 2
        x = jnp.minimum(x[:half], x[half:])
    return x


def _experts_body(xft_ref, u_ref, vt_ref, e1_ref, e2_ref, th_ref, x2_ref, gfin_ref,
                  y_ref, acc_ref, gate_ref, coef_even_ref, coef_odd_ref, *, n_tiles, final_norm):
    s = pl.program_id(0)
    last = pl.num_programs(0) - 1
    n_groups, d, tg = xft_ref.shape
    te = u_ref.shape[0]
    keys_per_step = te // N_KEYS
    j = s % n_tiles
    key0 = j * keys_per_step
    slab_rows_per_key = d // keys_per_step
    assert n_tiles % 2 == 0

    def project_values(c, prev_ref, rows):
        step = min(ACC_SLAB, rows.stop - rows.start)
        for r in range(rows.start, rows.stop, step):
            acc_ref[c, r:r + step, :] += lax.dot_general(
                vt_ref[:, r:r + step], prev_ref[c], (((0,), (0,)), ((), ())),
                preferred_element_type=F32)

    def route_weights(c, k):
        rows = slice(k * N_KEYS, (k + 1) * N_KEYS)
        e1_rows = [e1_ref[c, h, pl.ds(key0 + k, 1), :] for h in range(PEER_HEADS)]
        low = None
        for lg in range(tg // LANES):
            lanes = slice(lg * LANES, (lg + 1) * LANES)
            gate = None
            for h in range(PEER_HEADS):
                prod = _rows_bf16(e1_rows[h][:, lanes]) * e2_ref[c, h, :, lanes]
                sel = jnp.where(prod >= _rows_bf16(th_ref[c, h:h + 1, lanes]), prod, 0.0)
                gate = sel if gate is None else gate + sel
            gate_ref[c, rows, lanes] = gate
            low = gate if low is None else jnp.minimum(low, gate)
        return _fold_rows_min(low, 2 * SUBLANES)

    def activations(c):
        x = jnp.dot(u_ref[...], xft_ref[c], preferred_element_type=F32)
        return _gelu_twice(x).astype(BF16)

    def steady(prev_ref, next_ref):
        keys_per_slab = max(1, ACC_SLAB // slab_rows_per_key)
        for c in range(n_groups):
            act = activations(c)
            for k0 in range(0, keys_per_step, keys_per_slab):
                low = route_weights(c, k0)
                for k in range(k0 + 1, k0 + keys_per_slab):
                    low = jnp.minimum(low, route_weights(c, k))
                if c or k0:
                    prev_ref[c, 0:2 * SUBLANES, 0:LANES] += jnp.minimum(low, 0.0)
                project_values(c, prev_ref, slice(k0 * slab_rows_per_key,
                                                  (k0 + keys_per_slab) * slab_rows_per_key))
            next_ref[c] = gate_ref[c] * act

    @pl.when(jnp.logical_and(j != 0, j % 2 == 0))
    def _():
        steady(coef_odd_ref, coef_even_ref)

    @pl.when(j % 2 == 1)
    def _():
        steady(coef_even_ref, coef_odd_ref)

    @pl.when(jnp.logical_and(j == 0, s > 0))
    def _():
        for c in range(n_groups):
            project_values(c, coef_odd_ref, slice(0, d))
        for c in range(n_groups):
            rows = slice(c * tg, (c + 1) * tg)
            x3 = x2_ref[rows, :] + acc_ref[c].T
            y_ref[rows, :] = _rms_norm(x3, gfin_ref[...]) if final_norm else x3

    @pl.when(jnp.logical_and(j == 0, s < last))
    def _():
        acc_ref[...] = jnp.zeros(acc_ref.shape, F32)
        for c in range(n_groups):
            for k in range(keys_per_step):
                route_weights(c, k)
            coef_even_ref[c] = gate_ref[c] * activations(c)


def _experts(xft, u_bf, vt_bf, e1, e2, th, x2, gfin, *, groups_per_tile, te, final_norm):
    t, d = x2.shape
    tg = xft.shape[2]
    gpt = groups_per_tile
    tm = gpt * tg
    n_exp = u_bf.shape[0]
    n_tiles = n_exp // te
    n_tok = t // tm
    n_steps = n_tok * n_tiles + 1

    def tok_now(s):
        return jnp.minimum(s // n_tiles, n_tok - 1)

    def tok_prev(s):
        return jnp.maximum(s - 1, 0) // n_tiles

    return pl.pallas_call(
        functools.partial(_experts_body, n_tiles=n_tiles, final_norm=final_norm),
        grid=(n_steps,),
        in_specs=[
            pl.BlockSpec((gpt, d, tg), lambda s: (tok_now(s), 0, 0)),
            pl.BlockSpec((te, d), lambda s: (s % n_tiles, 0)),
            pl.BlockSpec((te, d), lambda s: (jnp.maximum(s - 1, 0) % n_tiles, 0)),
            pl.BlockSpec((gpt, PEER_HEADS, N_KEYS, tg), lambda s: (tok_now(s), 0, 0, 0)),
            pl.BlockSpec((gpt, PEER_HEADS, N_KEYS, tg), lambda s: (tok_now(s), 0, 0, 0)),
            pl.BlockSpec((gpt, PEER_HEADS, tg), lambda s: (tok_now(s), 0, 0)),
            pl.BlockSpec((tm, d), lambda s: (tok_prev(s), 0)),
            pl.BlockSpec((1, d), lambda s: (0, 0)),
        ],
        out_specs=pl.BlockSpec((tm, d), lambda s: (tok_prev(s), 0)),
        out_shape=jax.ShapeDtypeStruct((t, d), F32),
        scratch_shapes=[pltpu.VMEM((gpt, d, tg), F32), pltpu.VMEM((gpt, te, tg), BF16),
                        pltpu.VMEM((gpt, te, tg), BF16), pltpu.VMEM((gpt, te, tg), BF16)],
        compiler_params=pltpu.CompilerParams(
            dimension_semantics=("arbitrary",), vmem_limit_bytes=VMEM_LIMIT),
        name="peer_experts",
    )(xft, u_bf, vt_bf, e1, e2, th, x2, gfin.reshape(1, d))


def _pick_tile(n, pref):
    t = pref
    while n % t:
        t //= 2
    return t


def kernel(x_prompt, x_sample, state_conv, state_lru, g_mix, w_in, gmlp_ws, gmlp_bs, conv_w, conv_b,
           lru_wa, lru_ba, lru_wx, lru_bx, lru_lambda, w_out, g_ffn, peer_wq, peer_keys, peer_u,
           peer_v, g_final):
    batch, seq, d = x_prompt.shape
    dec_batch, dec_seq, _ = x_sample.shape
    depth = w_in.shape[0]
    assert batch == 1 and seq % GMLP_CHUNK == 0 and dec_seq <= CAUSAL_CHUNK
    assert d // LANES == gmlp_ws.shape[1] == lru_wa.shape[1]
    tp, ts = batch * seq, dec_batch * dec_seq

    idx = jnp.arange(GMLP_CHUNK)
    mask = (idx[:, None] // CAUSAL_CHUNK) >= (idx[None, :] // CAUSAL_CHUNK)
    hk = PEER_HEADS * N_KEYS

    xp = x_prompt.reshape(tp, d)
    xs = x_sample.reshape(ts, d)
    conv_p, lru_p, conv_s, lru_s, vrows_s = [], [], [], [], []
    for l in range(depth):
        last = l == depth - 1
        w_in_bf = w_in[l].astype(BF16)
        ws_p = (gmlp_ws[l] * mask.astype(F32)).astype(BF16)
        ws_s = ws_p[:, :dec_seq, :dec_seq]
        bias = jnp.repeat(gmlp_bs[l].T, LANES, axis=1)
        shared = (conv_w[l], conv_b[l].reshape(1, d), lru_wa[l].astype(BF16), lru_ba[l].reshape(1, d),
                  lru_wx[l].astype(BF16), lru_bx[l].reshape(1, d), lru_lambda[l].reshape(1, d),
                  w_out[l].astype(BF16))
        wqt = peer_wq[l].reshape(d, PEER_HEADS, 2, N_KEYS).transpose(2, 1, 3, 0).reshape(2 * hk, d)
        wqt = wqt.astype(BF16)
        keys_bf = peer_keys[l].astype(BF16)
        u_bf = peer_u[l].astype(BF16)
        vt_bf = peer_v[l].astype(BF16)
        gfin = g_final if last else jnp.ones((d,), F32)

        def ffn(x2):
            t = x2.shape[0]
            tm, tg = _pick_tile(t, 512), _pick_tile(t, 256)
            xft, e1, e2, th = _route(x2, g_ffn[l], wqt, keys_bf, tm=tg)
            return _experts(xft, u_bf, vt_bf, e1, e2, th, x2, gfin, groups_per_tile=tm // tg,
                            te=1024, final_norm=last)

        zp = _inproj(_prenorm(xp, g_mix[l], tm=_pick_tile(tp, 1024)), w_in_bf,
                     tm=_pick_tile(tp, 1024))
        zs = _inproj(_prenorm(xs, g_mix[l], tm=_pick_tile(ts, 1024)), w_in_bf,
                     tm=_pick_tile(ts, 1024))
        x2p, cp, lp = _mixer(zp, xp, 0, tp // GMLP_CHUNK, GMLP_CHUNK, ws_p, bias, *shared,
                             jnp.zeros((1, CONV_W - 1, d), F32), jnp.zeros((1, 1, d), F32),
                             carried=True)
        x2s, cs, ls = _mixer(zs, xs, 0, dec_batch, dec_seq, ws_s, bias[:dec_seq], *shared,
                             state_conv[l], state_lru[l].reshape(dec_batch, 1, d), carried=False)
        xp, xs = ffn(x2p), ffn(x2s)

        conv_p.append(cp)
        lru_p.append(lp.reshape(batch, d))
        conv_s.append(cs)
        lru_s.append(ls.reshape(dec_batch, d))
        vrows_s.append(zs[:, d:2 * d].reshape(dec_batch, dec_seq, d))

    return (xp.reshape(batch, seq, d), xs.reshape(dec_batch, dec_seq, d), jnp.stack(conv_p),
            jnp.stack(lru_p), jnp.stack(conv_s), jnp.stack(lru_s), jnp.stack(vrows_s))
```
